```python
import math
import jax, jax.numpy as jnp
from jax import lax
import numpy as np

D_MODEL = 1024
BATCH = 2
SEQ = 8192
DEPTH = 2

CHUNK = 64
N_MIXERS = 2
N_LAYERS_A = (DEPTH + N_MIXERS - 1) // N_MIXERS
N_LAYERS_B = DEPTH // N_MIXERS

A_HEADS = 8
A_HEAD_DIM = 128
A_WIDTH = A_HEADS * A_HEAD_DIM
CONV_WIDTH = 4
A_IN_COLS = 4 * A_WIDTH + 2 * A_HEADS

B_HEADS = 8
B_HEAD_DIM = 128
B_WIDTH = B_HEADS * B_HEAD_DIM
B_IN_COLS = 4 * B_WIDTH + B_HEADS
Q_BLOCK = 128

EPS = 1e-6

kernel_name = "hybrid_gdn_fox_interleaved"


def rms_norm(x, w):
    xf = x.astype(jnp.float32)
    y = xf * lax.rsqrt(jnp.mean(xf * xf, axis=-1, keepdims=True) + EPS)
    return (y * w.astype(jnp.float32)).astype(x.dtype)


def l2_normalize(x):
    xf = x.astype(jnp.float32)
    return xf * lax.rsqrt(jnp.sum(xf * xf, axis=-1, keepdims=True) + EPS)


def causal_depthwise_conv(x, w):
    c = x.shape[-1]
    return lax.conv_general_dilated(
        x, w[:, None, :].astype(x.dtype), window_strides=(1,),
        padding=[(CONV_WIDTH - 1, 0)], dimension_numbers=("NWC", "WIO", "NWC"),
        feature_group_count=c)


def chunk_gated_delta_rule(q, k, v, beta, g_log):
    b_, t_, h_, dk = q.shape
    dv = v.shape[-1]
    n = t_ // CHUNK

    def to_chunks(t):
        t = t.reshape((b_, n, CHUNK, h_) + t.shape[3:])
        return jnp.moveaxis(t, (1, 3), (0, 2))

    q = to_chunks(q) * (dk ** -0.5)
    k = to_chunks(k)
    v = to_chunks(v)
    beta = to_chunks(beta)
    g = jnp.cumsum(to_chunks(g_log), axis=-1)

    idx = jnp.arange(CHUNK)
    incl = idx[:, None] >= idx[None, :]
    strict = idx[:, None] > idx[None, :]
    gdiff = g[..., :, None] - g[..., None, :]
    decay_incl = jnp.exp(jnp.where(incl, gdiff, -jnp.inf))
    decay_strict = jnp.where(strict, decay_incl, 0.0)

    kb = k * beta[..., None]
    a_mat = jnp.einsum("nbhid,nbhjd->nbhij", kb, k) * decay_strict
    eye = jnp.eye(CHUNK, dtype=jnp.float32)
    t_mat = lax.linalg.triangular_solve(
        eye + a_mat, jnp.broadcast_to(eye, a_mat.shape),
        left_side=True, lower=True, unit_diagonal=True)
    u = t_mat @ (v * beta[..., None])
    w = t_mat @ (kb * jnp.exp(g)[..., None])
    attn_intra = jnp.einsum("nbhid,nbhjd->nbhij", q, k) * decay_incl
    q_decayed = q * jnp.exp(g)[..., None]
    k_to_end = k * jnp.exp(g[..., -1:] - g)[..., None]
    g_end = jnp.exp(g[..., -1])

    def step(state, xs):
        u_c, w_c, qd_c, kend_c, attn_c, gend_c = xs
        v_new = u_c - w_c @ state
        o_c = qd_c @ state + attn_c @ v_new
        state = state * gend_c[..., None, None] + jnp.swapaxes(kend_c, -1, -2) @ v_new
        return state, o_c

    s0 = jnp.zeros((b_, h_, dk, dv), jnp.float32)
    _, o = lax.scan(step, s0, (u, w, q_decayed, k_to_end, attn_intra, g_end))
    o = jnp.moveaxis(o, (0, 2), (1, 3))
    return o.reshape(b_, t_, h_, dv)


def gated_deltanet_mixer(h, w_in, conv_w, a_log, dt_bias, o_norm_w, w_out):
    b_, t_, _ = h.shape
    proj = h @ w_in
    qkv, z, b_raw, a_raw = jnp.split(
        proj, [3 * A_WIDTH, 4 * A_WIDTH, 4 * A_WIDTH + A_HEADS], axis=-1)
    qkv = jax.nn.silu(causal_depthwise_conv(qkv, conv_w))
    q, k, v = jnp.split(qkv, 3, axis=-1)
    hs = (b_, t_, A_HEADS, A_HEAD_DIM)
    q = l2_normalize(q.reshape(hs))
    k = l2_normalize(k.reshape(hs))
    v = v.reshape(hs).astype(jnp.float32)
    beta = jax.nn.sigmoid(b_raw.astype(jnp.float32))
    g_log = -jnp.exp(a_log.astype(jnp.float32)) * jax.nn.softplus(
        a_raw.astype(jnp.float32) + dt_bias.astype(jnp.float32))
    o = chunk_gated_delta_rule(q, k, v, beta, g_log)
    o = rms_norm(o, o_norm_w).astype(h.dtype)
    y = o * jax.nn.silu(z).reshape(hs)
    return y.reshape(b_, t_, A_WIDTH) @ w_out


def forgetting_attention_mixer(h, w_in, f_bias, q_norm_w, k_norm_w, w_out):
    b_, t_, _ = h.shape
    proj = h @ w_in
    q, k, v, z, f_raw = jnp.split(
        proj, [B_WIDTH, 2 * B_WIDTH, 3 * B_WIDTH, 4 * B_WIDTH], axis=-1)
    hs = (b_, t_, B_HEADS, B_HEAD_DIM)
    q = rms_norm(q.reshape(hs), q_norm_w)
    k = rms_norm(k.reshape(hs), k_norm_w)
    v = v.reshape(hs)
    log_f = jax.nn.log_sigmoid(f_raw.astype(jnp.float32) + f_bias.astype(jnp.float32))
    c = jnp.transpose(jnp.cumsum(log_f, axis=1), (0, 2, 1))
    qh = jnp.transpose(q, (0, 2, 1, 3))
    kh = jnp.transpose(k, (0, 2, 1, 3))
    vh = jnp.transpose(v, (0, 2, 1, 3))
    n_blocks = t_ // Q_BLOCK
    q_blocks = jnp.moveaxis(qh.reshape(b_, B_HEADS, n_blocks, Q_BLOCK, B_HEAD_DIM), 2, 0)
    c_blocks = jnp.moveaxis(c.reshape(b_, B_HEADS, n_blocks, Q_BLOCK), 2, 0)
    key_pos = jnp.arange(t_)
    scale = B_HEAD_DIM ** -0.5

    def attend(args):
        qb, cb, blk = args
        s = jnp.einsum("bhqd,bhkd->bhqk", qb, kh).astype(jnp.float32) * scale
        s = s + cb[..., :, None] - c[..., None, :]
        q_pos = blk * Q_BLOCK + jnp.arange(Q_BLOCK)
        mask = key_pos[None, :] <= q_pos[:, None]
        p = jax.nn.softmax(jnp.where(mask, s, -jnp.inf), axis=-1)
        return jnp.einsum("bhqk,bhkd->bhqd", p.astype(vh.dtype), vh)

    o = lax.map(attend, (q_blocks, c_blocks, jnp.arange(n_blocks)))
    o = jnp.moveaxis(o, (0, 2), (1, 3)).reshape(hs)
    y = o * jax.nn.silu(z).reshape(hs)
    return y.reshape(b_, t_, B_WIDTH) @ w_out


def setup_inputs(seed: int = 0) -> dict:
    key = jax.random.key(seed)
    ks = jax.random.split(key, 16)
    f32 = jnp.float32
    x = jax.random.normal(ks[0], (BATCH, SEQ, D_MODEL), f32)
    a_norm_w = 1.0 + 0.02 * jax.random.normal(ks[1], (N_LAYERS_A, D_MODEL), f32)
    a_w_in = jax.random.normal(ks[2], (N_LAYERS_A, D_MODEL, A_IN_COLS), f32) * D_MODEL ** -0.5
    a_conv_w = jax.random.normal(ks[3], (N_LAYERS_A, CONV_WIDTH, 3 * A_WIDTH), f32) * CONV_WIDTH ** -0.5
    a_A_log = jnp.log(jax.random.uniform(ks[4], (N_LAYERS_A, A_HEADS), f32, 1.0, 16.0))
    dt = jnp.exp(jax.random.uniform(ks[5], (N_LAYERS_A, A_HEADS), f32,
                                    math.log(1e-3), math.log(1e-1)))
    a_dt_bias = dt + jnp.log(-jnp.expm1(-dt))
    a_o_norm_w = 1.0 + 0.02 * jax.random.normal(ks[6], (N_LAYERS_A, A_HEAD_DIM), f32)
    a_w_out = jax.random.normal(ks[7], (N_LAYERS_A, A_WIDTH, D_MODEL), f32) * A_WIDTH ** -0.5
    b_norm_w = 1.0 + 0.02 * jax.random.normal(ks[8], (N_LAYERS_B, D_MODEL), f32)
    b_w_in = jax.random.normal(ks[9], (N_LAYERS_B, D_MODEL, B_IN_COLS), f32) * D_MODEL ** -0.5
    b_f_bias = 3.0 + 0.5 * jax.random.normal(ks[10], (N_LAYERS_B, B_HEADS), f32)
    b_q_norm_w = 1.0 + 0.02 * jax.random.normal(ks[11], (N_LAYERS_B, B_HEAD_DIM), f32)
    b_k_norm_w = 1.0 + 0.02 * jax.random.normal(ks[12], (N_LAYERS_B, B_HEAD_DIM), f32)
    b_w_out = jax.random.normal(ks[13], (N_LAYERS_B, B_WIDTH, D_MODEL), f32) * B_WIDTH ** -0.5
    final_norm_w = 1.0 + 0.02 * jax.random.normal(ks[14], (D_MODEL,), f32)
    return {"x": x, "a_norm_w": a_norm_w, "a_w_in": a_w_in, "a_conv_w": a_conv_w,
            "a_A_log": a_A_log, "a_dt_bias": a_dt_bias, "a_o_norm_w": a_o_norm_w,
            "a_w_out": a_w_out, "b_norm_w": b_norm_w, "b_w_in": b_w_in,
            "b_f_bias": b_f_bias, "b_q_norm_w": b_q_norm_w, "b_k_norm_w": b_k_norm_w,
            "b_w_out": b_w_out, "final_norm_w": final_norm_w}


def reference(x, a_norm_w, a_w_in, a_conv_w, a_A_log, a_dt_bias, a_o_norm_w, a_w_out,
              b_norm_w, b_w_in, b_f_bias, b_q_norm_w, b_k_norm_w, b_w_out, final_norm_w):
    h = x
    for i in range(DEPTH):
        j = i // N_MIXERS
        if i % N_MIXERS == 0:
            h = h + gated_deltanet_mixer(rms_norm(h, a_norm_w[j]), a_w_in[j], a_conv_w[j],
                                         a_A_log[j], a_dt_bias[j], a_o_norm_w[j], a_w_out[j])
        else:
            h = h + forgetting_attention_mixer(rms_norm(h, b_norm_w[j]), b_w_in[j], b_f_bias[j],
                                               b_q_norm_w[j], b_k_norm_w[j], b_w_out[j])
    return rms_norm(h, final_norm_w)
```

```python
import functools

import jax
import jax.numpy as jnp
from jax import lax
from jax.experimental import pallas as pl
from jax.experimental.pallas import tpu as pltpu

F32 = jnp.float32
BF16 = jnp.bfloat16
EPS = 1e-6

HEADS = 8
HEAD_DIM = 128
WIDTH = HEADS * HEAD_DIM
CONV_WIDTH = 4
N_MIXERS = 2

LANES = 128
SUBLANES = 8
VMEM_LIMIT_BYTES = 56 * 1024 * 1024

GDN_CHUNK = 256
GDN_HEADS_PER_STEP = 2
GDN_BASE_BLOCK = 16
PROJ_ROWS = 512
PREP_ROWS = 256
ATTN_BLOCK = 512

NT_DIMS = (((1,), (1,)), ((), ()))


def _sigmoid(x):
    return 1.0 / (1.0 + jnp.exp(-x))


def _softplus(x):
    return jnp.maximum(x, 0.0) + jnp.log1p(jnp.exp(-jnp.abs(x)))


def _mm(a, b):
    return jnp.dot(a.astype(BF16), b.astype(BF16), preferred_element_type=F32)


def _mm_nt(a, b):
    return lax.dot_general(a.astype(BF16), b.astype(BF16), NT_DIMS, preferred_element_type=F32)


def _norm_matmul_kernel(x_ref, nw_ref, w_ref, o_ref):
    x = x_ref[...]
    ms = jnp.mean(x * x, axis=-1, keepdims=True)
    hn = (x * lax.rsqrt(ms + EPS) * nw_ref[...]).astype(BF16)
    o_ref[...] = jnp.dot(hn, w_ref[...], preferred_element_type=F32)


def _norm_matmul(x, nw, w):
    n, d = x.shape
    ncol = w.shape[1]
    tm = PROJ_ROWS
    return pl.pallas_call(
        _norm_matmul_kernel,
        grid=(n // tm,),
        in_specs=[
            pl.BlockSpec((tm, d), lambda i: (i, 0)),
            pl.BlockSpec((1, d), lambda i: (0, 0)),
            pl.BlockSpec((d, ncol), lambda i: (0, 0)),
        ],
        out_specs=pl.BlockSpec((tm, ncol), lambda i: (i, 0)),
        out_shape=jax.ShapeDtypeStruct((n, ncol), F32),
        compiler_params=pltpu.CompilerParams(
            dimension_semantics=("parallel",), vmem_limit_bytes=VMEM_LIMIT_BYTES),
        name="norm_matmul",
    )(x, nw, w)


def _proj_residual_kernel(y_ref, r_ref, w_ref, o_ref):
    o_ref[...] = r_ref[...] + jnp.dot(y_ref[...].astype(BF16), w_ref[...],
                                      preferred_element_type=F32)


def _proj_residual_norm_kernel(y_ref, r_ref, w_ref, nw_ref, o_ref):
    h = r_ref[...] + jnp.dot(y_ref[...].astype(BF16), w_ref[...], preferred_element_type=F32)
    ms = jnp.mean(h * h, axis=-1, keepdims=True)
    o_ref[...] = h * lax.rsqrt(ms + EPS) * nw_ref[...]


def _proj_residual(y, res, w, final_nw=None):
    n, d_in = y.shape
    d_out = w.shape[1]
    tm = PROJ_ROWS
    in_specs = [
        pl.BlockSpec((tm, d_in), lambda i: (i, 0)),
        pl.BlockSpec((tm, d_out), lambda i: (i, 0)),
        pl.BlockSpec((d_in, d_out), lambda i: (0, 0)),
    ]
    args = [y, res, w]
    body = _proj_residual_kernel
    if final_nw is not None:
        in_specs.append(pl.BlockSpec((1, d_out), lambda i: (0, 0)))
        args.append(final_nw)
        body = _proj_residual_norm_kernel
    return pl.pallas_call(
        body,
        grid=(n // tm,),
        in_specs=in_specs,
        out_specs=pl.BlockSpec((tm, d_out), lambda i: (i, 0)),
        out_shape=jax.ShapeDtypeStruct((n, d_out), F32),
        compiler_params=pltpu.CompilerParams(
            dimension_semantics=("parallel",), vmem_limit_bytes=VMEM_LIMIT_BYTES),
        name="proj_residual",
    )(*args)


def _conv_silu(x, tail, w):
    c = x.shape[0]
    row8 = lax.broadcasted_iota(jnp.int32, (SUBLANES, x.shape[1]), 0)
    acc = x * w[CONV_WIDTH - 1:CONV_WIDTH, :]
    for s in range(1, CONV_WIDTH):
        xs = pltpu.roll(x, s, 0)
        ts = pltpu.roll(tail, s, 0)
        head = jnp.where(row8 < s, ts, xs[0:SUBLANES, :])
        xs = jnp.concatenate([head, xs[SUBLANES:c, :]], axis=0)
        acc = acc + xs * w[CONV_WIDTH - 1 - s:CONV_WIDTH - s, :]
    return acc * _sigmoid(acc)


def _unit_lower_inverse_minus_identity(a, xr):
    c = a.shape[0]
    b = GDN_BASE_BLOCK
    a_blk = jnp.where(xr < b, a, 0.0)
    n = -a_blk
    p = a_blk
    width = 2
    while width < b:
        p = _mm(p, p)
        n = n + p + _mm(n, p)
        width *= 2
    while b < c:
        a_next = jnp.where(xr < 2 * b, a, 0.0) if 2 * b < c else a
        e = a_next - a_blk
        y = e + _mm(n, e)
        n = n - y - _mm(y, n)
        a_blk = a_next
        b *= 2
    return n


def _gdn_kernel(q_ref, k_ref, v_ref, z_ref, ba_ref, cwq_ref, cwk_ref, cwv_ref, prm_ref, onw_ref,
                y_ref, s_ref, tail_ref, *, hg):
    c = q_ref.shape[0]

    @pl.when(pl.program_id(2) == 0)
    def _():
        s_ref[...] = jnp.zeros_like(s_ref)
        tail_ref[...] = jnp.zeros_like(tail_ref)

    xq = q_ref[...]
    xk = k_ref[...]
    xv = v_ref[...]
    qc = _conv_silu(xq, tail_ref[0], cwq_ref[...])
    kc = _conv_silu(xk, tail_ref[1], cwk_ref[...])
    vc = _conv_silu(xv, tail_ref[2], cwv_ref[...])
    tail_ref[0] = xq[c - SUBLANES:c, :]
    tail_ref[1] = xk[c - SUBLANES:c, :]
    tail_ref[2] = xv[c - SUBLANES:c, :]

    row = lax.broadcasted_iota(jnp.int32, (c, c), 0)
    col = lax.broadcasted_iota(jnp.int32, (c, c), 1)
    lower_incl = row >= col
    lower_strict = row > col
    xr = row ^ col

    ba = ba_ref[...]
    neg_a = -jnp.exp(prm_ref[0:1, :])
    beta_all = _sigmoid(ba)
    glog = neg_a * _softplus(ba + prm_ref[1:2, :])
    gcum = jnp.dot(lower_incl.astype(F32), glog, precision=lax.Precision.HIGHEST,
                   preferred_element_type=F32)
    gcum_t = gcum.T

    for i in range(hg):
        sl = slice(i * HEAD_DIM, (i + 1) * HEAD_DIM)
        q = qc[:, sl]
        k = kc[:, sl]
        v = vc[:, sl]
        qn = q * (lax.rsqrt(jnp.sum(q * q, axis=-1, keepdims=True) + EPS) * (HEAD_DIM ** -0.5))
        kn = k * lax.rsqrt(jnp.sum(k * k, axis=-1, keepdims=True) + EPS)
        beta = beta_all[:, i:i + 1]
        gi = SUBLANES + i
        gcol = gcum[:, gi:gi + 1]
        grow = gcum_t[gi:gi + 1, :]
        glast = gcum[c - 1:c, gi:gi + 1]
        eg = jnp.exp(gcol)
        kb = kn * beta
        vb = v * beta
        kbg = kb * eg
        qd = qn * eg
        kend = kn * jnp.exp(glast - gcol)
        dec = jnp.exp(jnp.where(lower_incl, gcol - grow, -jnp.inf))
        kn_b = kn.astype(BF16)
        a = jnp.where(lower_strict, _mm_nt(kb, kn_b) * dec, 0.0)
        attn = _mm_nt(qn, kn_b) * dec
        n = _unit_lower_inverse_minus_identity(a, xr)
        uw = _mm(n, jnp.concatenate([vb, kbg], axis=1))
        u = vb + uw[:, :HEAD_DIM]
        w = kbg + uw[:, HEAD_DIM:]
        s = s_ref[i]
        s_b = s.astype(BF16)
        v_new = u - _mm(w, s_b)
        v_new_b = v_new.astype(BF16)
        o = _mm(qd, s_b) + _mm(attn, v_new_b)
        s_ref[i] = s * jnp.exp(glast) + _mm(kend.T, v_new_b)
        on = o * lax.rsqrt(jnp.mean(o * o, axis=-1, keepdims=True) + EPS) * onw_ref[...]
        z = z_ref[:, sl]
        y_ref[:, sl] = on * (z * _sigmoid(z))


def _gdn(proj, cw_q, cw_k, cw_v, prm, onw, batch, seq):
    n = proj.shape[0]
    c = GDN_CHUNK
    hg = GDN_HEADS_PER_STEP
    gw = hg * HEAD_DIM
    groups = HEADS // hg
    nt = seq // c
    small0 = 4 * WIDTH // LANES

    def col_spec(base):
        return pl.BlockSpec((c, gw), lambda b, g, t: (b * nt + t, base + g))

    def cw_spec():
        return pl.BlockSpec((SUBLANES, gw), lambda b, g, t: (0, g))

    return pl.pallas_call(
        functools.partial(_gdn_kernel, hg=hg),
        grid=(batch, groups, nt),
        in_specs=[
            col_spec(0), col_spec(groups), col_spec(2 * groups), col_spec(3 * groups),
            pl.BlockSpec((c, LANES), lambda b, g, t: (b * nt + t, small0 + g)),
            cw_spec(), cw_spec(), cw_spec(),
            pl.BlockSpec((None, SUBLANES, LANES), lambda b, g, t: (g, 0, 0)),
            pl.BlockSpec((1, HEAD_DIM), lambda b, g, t: (0, 0)),
        ],
        out_specs=pl.BlockSpec((c, gw), lambda b, g, t: (b * nt + t, g)),
        out_shape=jax.ShapeDtypeStruct((n, WIDTH), F32),
        scratch_shapes=[
            pltpu.VMEM((hg, HEAD_DIM, HEAD_DIM), F32),
            pltpu.VMEM((3, SUBLANES, gw), F32),
        ],
        compiler_params=pltpu.CompilerParams(
            dimension_semantics=("parallel", "parallel", "arbitrary"),
            vmem_limit_bytes=VMEM_LIMIT_BYTES),
        name="gdn",
    )(proj, proj, proj, proj, proj, cw_q, cw_k, cw_v, prm, onw)


def _fox_prep_kernel(q_ref, k_ref, v_ref, f_ref, fb_ref, qw_ref, kw_ref,
                     qo_ref, ko_ref, vo_ref, nc_ref, carry_ref):
    tm = q_ref.shape[0]

    @pl.when(pl.program_id(1) == 0)
    def _():
        carry_ref[...] = jnp.zeros_like(carry_ref)

    f = f_ref[...] + fb_ref[...]
    logf = jnp.minimum(f, 0.0) - jnp.log1p(jnp.exp(-jnp.abs(f)))
    row = lax.broadcasted_iota(jnp.int32, (tm, tm), 0)
    col = lax.broadcasted_iota(jnp.int32, (tm, tm), 1)
    cum = jnp.dot((row >= col).astype(F32), logf, precision=lax.Precision.HIGHEST,
                  preferred_element_type=F32) + carry_ref[...]
    carry_ref[...] = cum[tm - 1:tm, :]
    nc_ref[...] = (-cum).T[0:HEADS, :]

    lane = lax.broadcasted_iota(jnp.int32, (tm, LANES), 1)
    ones_col = jnp.where(lane == 0, 1.0, 0.0).astype(BF16)
    scale = HEAD_DIM ** -0.5
    for h in range(HEADS):
        sl = slice(h * HEAD_DIM, (h + 1) * HEAD_DIM)
        q = q_ref[:, sl]
        k = k_ref[:, sl]
        qn = q * lax.rsqrt(jnp.mean(q * q, axis=-1, keepdims=True) + EPS) * qw_ref[...]
        kn = k * lax.rsqrt(jnp.mean(k * k, axis=-1, keepdims=True) + EPS) * kw_ref[...]
        qo_ref[h] = (qn * scale).astype(BF16)
        ko_ref[h] = kn.astype(BF16)
        vo_ref[h, :, 0:HEAD_DIM] = v_ref[:, sl].astype(BF16)
        vo_ref[h, :, HEAD_DIM:2 * HEAD_DIM] = ones_col


def _fox_prep(proj, fb, qw, kw, batch, seq):
    tm = PREP_ROWS
    nt = seq // tm
    wb = WIDTH // LANES

    def col_spec(base, width):
        return pl.BlockSpec((tm, width), lambda b, t: (b * nt + t, base))

    head_spec = pl.BlockSpec((None, HEADS, tm, HEAD_DIM), lambda b, t: (b, 0, t, 0))
    return pl.pallas_call(
        _fox_prep_kernel,
        grid=(batch, nt),
        in_specs=[
            col_spec(0, WIDTH), col_spec(1, WIDTH), col_spec(2, WIDTH),
            col_spec(4 * wb, LANES),
            pl.BlockSpec((1, LANES), lambda b, t: (0, 0)),
            pl.BlockSpec((1, HEAD_DIM), lambda b, t: (0, 0)),
            pl.BlockSpec((1, HEAD_DIM), lambda b, t: (0, 0)),
        ],
        out_specs=[
            head_spec, head_spec,
            pl.BlockSpec((None, HEADS, tm, 2 * HEAD_DIM), lambda b, t: (b, 0, t, 0)),
            pl.BlockSpec((None, HEADS, tm), lambda b, t: (b, 0, t)),
        ],
        out_shape=[
            jax.ShapeDtypeStruct((batch, HEADS, seq, HEAD_DIM), BF16),
            jax.ShapeDtypeStruct((batch, HEADS, seq, HEAD_DIM), BF16),
            jax.ShapeDtypeStruct((batch, HEADS, seq, 2 * HEAD_DIM), BF16),
            jax.ShapeDtypeStruct((batch, HEADS, seq), F32),
        ],
        scratch_shapes=[pltpu.VMEM((1, LANES), F32)],
        compiler_params=pltpu.CompilerParams(
            dimension_semantics=("parallel", "arbitrary"), vmem_limit_bytes=VMEM_LIMIT_BYTES),
        name="fox_prep",
    )(proj, proj, proj, proj, fb, qw, kw)


def _fox_attn_kernel(q_ref, k_ref, v_ref, nc_ref, z_ref, o_ref, m_ref, acc_ref, *, blk):
    qi = pl.program_id(1)
    q = q_ref[...]
    m_ref[...] = jnp.full_like(m_ref, -jnp.inf)
    acc_ref[...] = jnp.zeros_like(acc_ref)

    def step(j, masked):
        start = pl.multiple_of(j * blk, blk)
        k = k_ref[pl.ds(start, blk), :]
        v = v_ref[pl.ds(start, blk), :]
        s = lax.dot_general(q, k, NT_DIMS, preferred_element_type=F32) + nc_ref[:, pl.ds(start, blk)]
        if masked:
            row = lax.broadcasted_iota(jnp.int32, (blk, blk), 0)
            col = lax.broadcasted_iota(jnp.int32, (blk, blk), 1)
            s = jnp.where(row >= col, s, -jnp.inf)
        m_prev = m_ref[...]
        m_new = jnp.maximum(m_prev, jnp.max(s, axis=-1, keepdims=True))
        alpha = jnp.exp(m_prev - m_new)
        p = jnp.exp(s - jnp.tile(m_new, (1, blk // LANES)))
        acc_ref[...] = acc_ref[...] * jnp.tile(alpha, (1, 2)) + jnp.dot(
            p.astype(BF16), v, preferred_element_type=F32)
        m_ref[...] = m_new

    def body(j, carry):
        step(j, False)
        return carry

    lax.fori_loop(0, qi, body, 0)
    step(qi, True)
    acc = acc_ref[...]
    o = acc[:, :HEAD_DIM] / acc[:, HEAD_DIM:HEAD_DIM + 1]
    z = z_ref[...]
    o_ref[...] = o * (z * _sigmoid(z))


def _fox_attn(q, k, v, nc, proj, batch, seq):
    blk = ATTN_BLOCK
    nq = seq // blk
    bh = batch * HEADS
    z0 = 3 * WIDTH // HEAD_DIM
    return pl.pallas_call(
        functools.partial(_fox_attn_kernel, blk=blk),
        grid=(bh, nq),
        in_specs=[
            pl.BlockSpec((None, blk, HEAD_DIM), lambda g, i: (g, i, 0)),
            pl.BlockSpec((None, seq, HEAD_DIM), lambda g, i: (g, 0, 0)),
            pl.BlockSpec((None, seq, 2 * HEAD_DIM), lambda g, i: (g, 0, 0)),
            pl.BlockSpec((None, 1, seq), lambda g, i: (g, 0, 0)),
            pl.BlockSpec((blk, HEAD_DIM), lambda g, i: ((g // HEADS) * nq + i, z0 + g % HEADS)),
        ],
        out_specs=pl.BlockSpec((blk, HEAD_DIM), lambda g, i: ((g // HEADS) * nq + i, g % HEADS)),
        out_shape=jax.ShapeDtypeStruct((batch * seq, WIDTH), F32),
        scratch_shapes=[
            pltpu.VMEM((blk, LANES), F32),
            pltpu.VMEM((blk, 2 * HEAD_DIM), F32),
        ],
        compiler_params=pltpu.CompilerParams(
            dimension_semantics=("parallel", "arbitrary"), vmem_limit_bytes=VMEM_LIMIT_BYTES),
        name="fox_attn",
    )(q, k, v, nc, proj)


def _pad_cols(w, ncol):
    return jnp.pad(w, ((0, 0), (0, ncol - w.shape[1])))


def _gdn_in_weights(w_in):
    hg = GDN_HEADS_PER_STEP
    d = w_in.shape[0]
    main = w_in[:, :4 * WIDTH]
    wb = w_in[:, 4 * WIDTH:4 * WIDTH + HEADS]
    wa = w_in[:, 4 * WIDTH + HEADS:]
    blocks = []
    for g in range(HEADS // hg):
        blk = jnp.zeros((d, LANES), w_in.dtype)
        blk = blk.at[:, 0:hg].set(wb[:, g * hg:(g + 1) * hg])
        blk = blk.at[:, SUBLANES:SUBLANES + hg].set(wa[:, g * hg:(g + 1) * hg])
        blocks.append(blk)
    return jnp.concatenate([main] + blocks, axis=1).astype(BF16)


def _gdn_gate_params(a_log, dt_bias):
    hg = GDN_HEADS_PER_STEP
    groups = HEADS // hg
    prm = jnp.zeros((groups, SUBLANES, LANES), F32)
    prm = prm.at[:, 0, SUBLANES:SUBLANES + hg].set(a_log.reshape(groups, hg).astype(F32))
    prm = prm.at[:, 1, SUBLANES:SUBLANES + hg].set(dt_bias.reshape(groups, hg).astype(F32))
    return prm


def _conv_taps(conv_w):
    taps = jnp.pad(conv_w.astype(F32), ((0, SUBLANES - CONV_WIDTH), (0, 0)))
    return taps[:, :WIDTH], taps[:, WIDTH:2 * WIDTH], taps[:, 2 * WIDTH:]


def kernel(x, a_norm_w, a_w_in, a_conv_w, a_A_log, a_dt_bias, a_o_norm_w, a_w_out, b_norm_w, b_w_in, b_f_bias, b_q_norm_w, b_k_norm_w, b_w_out, final_norm_w):
    batch, seq, d_model = x.shape
    depth = a_norm_w.shape[0] + b_norm_w.shape[0]
    h = x.reshape(batch * seq, d_model)
    final_nw = final_norm_w.reshape(1, d_model)
    for i in range(depth):
        j = i // N_MIXERS
        last_nw = final_nw if i == depth - 1 else None
        if i % N_MIXERS == 0:
            proj = _norm_matmul(h, a_norm_w[j].reshape(1, d_model), _gdn_in_weights(a_w_in[j]))
            cw_q, cw_k, cw_v = _conv_taps(a_conv_w[j])
            y = _gdn(proj, cw_q, cw_k, cw_v, _gdn_gate_params(a_A_log[j], a_dt_bias[j]),
                     a_o_norm_w[j].reshape(1, HEAD_DIM), batch, seq)
            h = _proj_residual(y, h, a_w_out[j].astype(BF16), last_nw)
        else:
            ncol = 4 * WIDTH + LANES
            proj = _norm_matmul(h, b_norm_w[j].reshape(1, d_model),
                                _pad_cols(b_w_in[j], ncol).astype(BF16))
            fb = jnp.pad(b_f_bias[j].astype(F32), (0, LANES - HEADS)).reshape(1, LANES)
            q, k, v, nc = _fox_prep(proj, fb, b_q_norm_w[j].reshape(1, HEAD_DIM),
                                    b_k_norm_w[j].reshape(1, HEAD_DIM), batch, seq)
            bh = batch * HEADS
            y = _fox_attn(q.reshape(bh, seq, HEAD_DIM), k.reshape(bh, seq, HEAD_DIM),
                          v.reshape(bh, seq, 2 * HEAD_DIM), nc.reshape(bh, 1, seq),
                          proj, batch, seq)
            h = _proj_residual(y, h, b_w_out[j].astype(BF16), last_nw)
    return h.reshape(batch, seq, d_model)
```

```python
import functools

import jax
import jax.numpy as jnp
from jax import lax
from jax.experimental import pallas as pl
from jax.experimental.pallas import tpu as pltpu

F32 = jnp.float32
BF16 = jnp.bfloat16
EPS = 1e-6

HEADS = 8
HEAD_DIM = 128
WIDTH = HEADS * HEAD_DIM
CONV_WIDTH = 4
N_MIXERS = 2

LANES = 128
SUBLANES = 8
VMEM_LIMIT_BYTES = 56 * 1024 * 1024

GDN_CHUNK = 256
GDN_HEADS_PER_STEP = 8
GDN_BASE_BLOCK = 16
PROJ_ROWS = 512
PREP_ROWS = 256
ATTN_BLOCK = 512
ATTN_SUBBLOCKS = 2
LOG2E = 1.4426950408889634

NT_DIMS = (((1,), (1,)), ((), ()))


def _sigmoid(x):
    return 0.5 + 0.5 * jnp.tanh(0.5 * x)


def _silu(x):
    h = 0.5 * x
    return h + h * jnp.tanh(h)


def _softplus(x):
    return jnp.maximum(x, 0.0) + jnp.log1p(jnp.exp(-jnp.abs(x)))


def _mm(a, b):
    return jnp.dot(a.astype(BF16), b.astype(BF16), preferred_element_type=F32)


def _mm_nt(a, b):
    return lax.dot_general(a.astype(BF16), b.astype(BF16), NT_DIMS, preferred_element_type=F32)


def _norm_matmul_kernel(x_ref, nw_ref, w_ref, o_ref):
    x = x_ref[...]
    ms = jnp.mean(x * x, axis=-1, keepdims=True)
    hn = (x * lax.rsqrt(ms + EPS) * nw_ref[...]).astype(BF16)
    o_ref[...] = jnp.dot(hn, w_ref[...], preferred_element_type=F32)


def _norm_matmul(x, nw, w):
    n, d = x.shape
    ncol = w.shape[1]
    tm = PROJ_ROWS
    return pl.pallas_call(
        _norm_matmul_kernel,
        grid=(n // tm,),
        in_specs=[
            pl.BlockSpec((tm, d), lambda i: (i, 0)),
            pl.BlockSpec((1, d), lambda i: (0, 0)),
            pl.BlockSpec((d, ncol), lambda i: (0, 0)),
        ],
        out_specs=pl.BlockSpec((tm, ncol), lambda i: (i, 0)),
        out_shape=jax.ShapeDtypeStruct((n, ncol), F32),
        compiler_params=pltpu.CompilerParams(
            dimension_semantics=("parallel",), vmem_limit_bytes=VMEM_LIMIT_BYTES),
        name="norm_matmul",
    )(x, nw, w)


def _proj_residual_kernel(y_ref, r_ref, w_ref, o_ref):
    o_ref[...] = r_ref[...] + jnp.dot(y_ref[...].astype(BF16), w_ref[...],
                                      preferred_element_type=F32)


def _proj_residual_norm_kernel(y_ref, r_ref, w_ref, nw_ref, o_ref):
    h = r_ref[...] + jnp.dot(y_ref[...].astype(BF16), w_ref[...], preferred_element_type=F32)
    ms = jnp.mean(h * h, axis=-1, keepdims=True)
    o_ref[...] = h * lax.rsqrt(ms + EPS) * nw_ref[...]


def _proj_residual(y, res, w, final_nw=None):
    n, d_in = y.shape
    d_out = w.shape[1]
    tm = PROJ_ROWS
    in_specs = [
        pl.BlockSpec((tm, d_in), lambda i: (i, 0)),
        pl.BlockSpec((tm, d_out), lambda i: (i, 0)),
        pl.BlockSpec((d_in, d_out), lambda i: (0, 0)),
    ]
    args = [y, res, w]
    body = _proj_residual_kernel
    if final_nw is not None:
        in_specs.append(pl.BlockSpec((1, d_out), lambda i: (0, 0)))
        args.append(final_nw)
        body = _proj_residual_norm_kernel
    return pl.pallas_call(
        body,
        grid=(n // tm,),
        in_specs=in_specs,
        out_specs=pl.BlockSpec((tm, d_out), lambda i: (i, 0)),
        out_shape=jax.ShapeDtypeStruct((n, d_out), F32),
        compiler_params=pltpu.CompilerParams(
            dimension_semantics=("parallel",), vmem_limit_bytes=VMEM_LIMIT_BYTES),
        name="proj_residual",
    )(*args)


def _conv_silu(x_ref, tail, w):
    c = x_ref.shape[0]
    last = CONV_WIDTH - 1
    row8 = lax.broadcasted_iota(jnp.int32, (SUBLANES, x_ref.shape[1]), 0)
    x0 = x_ref[0:SUBLANES, :]
    lo = x0 * w[last:last + 1, :]
    hi = x_ref[SUBLANES:c, :] * w[last:last + 1, :]
    for s in range(1, CONV_WIDTH):
        tap = w[last - s:last - s + 1, :]
        lo = lo + jnp.where(row8 < s, pltpu.roll(tail, s, 0), pltpu.roll(x0, s, 0)) * tap
        hi = hi + x_ref[SUBLANES - s:c - s, :] * tap
    return _silu(jnp.concatenate([lo, hi], axis=0))


def _block_masks(c):
    row = lax.broadcasted_iota(jnp.int32, (c, c), 0)
    col = lax.broadcasted_iota(jnp.int32, (c, c), 1)
    xr = row ^ col
    eye = jnp.where(row == col, 1.0, 0.0).astype(BF16)
    b = GDN_BASE_BLOCK
    below = jnp.where(xr < b, 1.0, 0.0)
    masks = [below.astype(BF16)]
    while b < c:
        below_next = jnp.where(xr < 2 * b, 1.0, 0.0)
        masks.append((below_next - below).astype(BF16))
        below = below_next
        b *= 2
    return eye, masks


def _dot_b(a, b):
    return jnp.dot(a, b, preferred_element_type=F32).astype(BF16)


def _unit_lower_inverses(a_list, eye, masks):
    ps = [a * masks[0] for a in a_list]
    xs = [eye - p for p in ps]
    width = 2
    while width < GDN_BASE_BLOCK:
        ps = [_dot_b(p, p) for p in ps]
        xs = [x + _dot_b(x, p) for x, p in zip(xs, ps)]
        width *= 2
    for m in masks[1:]:
        ys = [_dot_b(x, a * m) for x, a in zip(xs, a_list)]
        xs = [x - _dot_b(y, x) for x, y in zip(xs, ys)]
    return xs


def _gdn_kernel(q_ref, k_ref, v_ref, z_ref, ba_ref, cwq_ref, cwk_ref, cwv_ref, prm_ref, onw_ref,
                y_ref, s_ref, tail_ref, *, hg):
    c = q_ref.shape[0]

    @pl.when(pl.program_id(2) == 0)
    def _():
        s_ref[...] = jnp.zeros_like(s_ref)
        tail_ref[...] = jnp.zeros_like(tail_ref)

    qc = _conv_silu(q_ref, tail_ref[0], cwq_ref[...])
    kc = _conv_silu(k_ref, tail_ref[1], cwk_ref[...])
    vc = _conv_silu(v_ref, tail_ref[2], cwv_ref[...])
    tail_ref[0] = q_ref[c - SUBLANES:c, :]
    tail_ref[1] = k_ref[c - SUBLANES:c, :]
    tail_ref[2] = v_ref[c - SUBLANES:c, :]

    row = lax.broadcasted_iota(jnp.int32, (c, c), 0)
    col = lax.broadcasted_iota(jnp.int32, (c, c), 1)
    lower_incl = row >= col
    lower_strict = row > col
    eye, masks = _block_masks(c)

    ba = ba_ref[...]
    neg_a = -jnp.exp(prm_ref[0:1, :])
    beta_all = _sigmoid(ba)
    glog = neg_a * _softplus(ba + prm_ref[1:2, :])
    gcum = jnp.dot(lower_incl.astype(F32), glog, precision=lax.Precision.HIGHEST,
                   preferred_element_type=F32)
    gcum_t = gcum.T

    heads = range(hg)
    sls = [slice(i * HEAD_DIM, (i + 1) * HEAD_DIM) for i in heads]
    gcols = [gcum[:, SUBLANES + i:SUBLANES + i + 1] for i in heads]
    glasts = [gcum[c - 1:c, SUBLANES + i:SUBLANES + i + 1] for i in heads]
    qn, kn_b, vb, kbg, qd, kend, a_l, attn = [], [], [], [], [], [], [], []
    for i in heads:
        q = qc[:, sls[i]]
        k = kc[:, sls[i]]
        qn_i = q * (lax.rsqrt(jnp.sum(q * q, axis=-1, keepdims=True) + EPS) * (HEAD_DIM ** -0.5))
        kn_i = k * lax.rsqrt(jnp.sum(k * k, axis=-1, keepdims=True) + EPS)
        beta = beta_all[:, i:i + 1]
        eg = jnp.exp(gcols[i])
        kb = kn_i * beta
        qn.append(qn_i)
        kn_b.append(kn_i.astype(BF16))
        vb.append(vc[:, sls[i]] * beta)
        kbg.append(kb * eg)
        qd.append(qn_i * eg)
        kend.append(kn_i * jnp.exp(glasts[i] - gcols[i]))
        grow = gcum_t[SUBLANES + i:SUBLANES + i + 1, :]
        dec = jnp.exp(jnp.where(lower_incl, gcols[i] - grow, -jnp.inf))
        a_l.append(jnp.where(lower_strict, _mm_nt(kb, kn_b[i]) * dec, 0.0).astype(BF16))
        attn.append((_mm_nt(qn_i, kn_b[i]) * dec).astype(BF16))
    ns = [x - eye for x in _unit_lower_inverses(a_l, eye, masks)]
    uws = [jnp.dot(ns[i], jnp.concatenate([vb[i], kbg[i]], axis=1).astype(BF16),
                   preferred_element_type=F32) for i in heads]
    us = [vb[i] + uws[i][:, :HEAD_DIM] for i in heads]
    wq = [jnp.concatenate([kbg[i] + uws[i][:, HEAD_DIM:], qd[i]], axis=0).astype(BF16)
          for i in heads]
    ss = [s_ref[i] for i in heads]
    s_bs = [s.astype(BF16) for s in ss]
    wss = [jnp.dot(wq[i], s_bs[i], preferred_element_type=F32) for i in heads]
    v_new_b = [(us[i] - wss[i][:c, :]).astype(BF16) for i in heads]
    os_ = [wss[i][c:, :] + jnp.dot(attn[i], v_new_b[i], preferred_element_type=F32)
           for i in heads]
    for i in heads:
        s_ref[i] = ss[i] * jnp.exp(glasts[i]) + _mm(kend[i].T, v_new_b[i])
    for i in heads:
        o = os_[i]
        on = o * lax.rsqrt(jnp.mean(o * o, axis=-1, keepdims=True) + EPS) * onw_ref[...]
        y_ref[:, sls[i]] = on * _silu(z_ref[:, sls[i]])


def _gdn(proj, cw_q, cw_k, cw_v, prm, onw, batch, seq):
    n = proj.shape[0]
    c = GDN_CHUNK
    hg = GDN_HEADS_PER_STEP
    gw = hg * HEAD_DIM
    groups = HEADS // hg
    nt = seq // c
    small0 = 4 * WIDTH // LANES

    def col_spec(base):
        return pl.BlockSpec((c, gw), lambda b, g, t: (b * nt + t, base + g))

    def cw_spec():
        return pl.BlockSpec((SUBLANES, gw), lambda b, g, t: (0, g))

    return pl.pallas_call(
        functools.partial(_gdn_kernel, hg=hg),
        grid=(batch, groups, nt),
        in_specs=[
            col_spec(0), col_spec(groups), col_spec(2 * groups), col_spec(3 * groups),
            pl.BlockSpec((c, LANES), lambda b, g, t: (b * nt + t, small0 + g)),
            cw_spec(), cw_spec(), cw_spec(),
            pl.BlockSpec((None, SUBLANES, LANES), lambda b, g, t: (g, 0, 0)),
            pl.BlockSpec((1, HEAD_DIM), lambda b, g, t: (0, 0)),
        ],
        out_specs=pl.BlockSpec((c, gw), lambda b, g, t: (b * nt + t, g)),
        out_shape=jax.ShapeDtypeStruct((n, WIDTH), F32),
        scratch_shapes=[
            pltpu.VMEM((hg, HEAD_DIM, HEAD_DIM), F32),
            pltpu.VMEM((3, SUBLANES, gw), F32),
        ],
        compiler_params=pltpu.CompilerParams(
            dimension_semantics=("parallel", "parallel", "arbitrary"),
            vmem_limit_bytes=VMEM_LIMIT_BYTES),
        name="gdn",
    )(proj, proj, proj, proj, proj, cw_q, cw_k, cw_v, prm, onw)


def _fox_prep_kernel(q_ref, k_ref, v_ref, f_ref, fb_ref, qw_ref, kw_ref,
                     qo_ref, ko_ref, vo_ref, nc_ref, carry_ref):
    tm = q_ref.shape[0]

    @pl.when(pl.program_id(1) == 0)
    def _():
        carry_ref[...] = jnp.zeros_like(carry_ref)

    f = f_ref[...] + fb_ref[...]
    logf = jnp.minimum(f, 0.0) - jnp.log1p(jnp.exp(-jnp.abs(f)))
    row = lax.broadcasted_iota(jnp.int32, (tm, tm), 0)
    col = lax.broadcasted_iota(jnp.int32, (tm, tm), 1)
    cum = jnp.dot((row >= col).astype(F32), logf, precision=lax.Precision.HIGHEST,
                  preferred_element_type=F32) + carry_ref[...]
    carry_ref[...] = cum[tm - 1:tm, :]
    nc_ref[...] = (cum * (-LOG2E)).T[0:HEADS, :]

    lane = lax.broadcasted_iota(jnp.int32, (tm, LANES), 1)
    ones_col = jnp.where(lane == 0, 1.0, 0.0).astype(BF16)
    scale = HEAD_DIM ** -0.5 * LOG2E
    for h in range(HEADS):
        sl = slice(h * HEAD_DIM, (h + 1) * HEAD_DIM)
        q = q_ref[:, sl]
        k = k_ref[:, sl]
        qn = q * lax.rsqrt(jnp.mean(q * q, axis=-1, keepdims=True) + EPS) * qw_ref[...]
        kn = k * lax.rsqrt(jnp.mean(k * k, axis=-1, keepdims=True) + EPS) * kw_ref[...]
        qo_ref[h] = (qn * scale).astype(BF16)
        ko_ref[h] = kn.astype(BF16)
        vo_ref[h, :, 0:HEAD_DIM] = v_ref[:, sl].astype(BF16)
        vo_ref[h, :, HEAD_DIM:2 * HEAD_DIM] = ones_col


def _fox_prep(proj, fb, qw, kw, batch, seq):
    tm = PREP_ROWS
    nt = seq // tm
    wb = WIDTH // LANES

    def col_spec(base, width):
        return pl.BlockSpec((tm, width), lambda b, t: (b * nt + t, base))

    head_spec = pl.BlockSpec((None, HEADS, tm, HEAD_DIM), lambda b, t: (b, 0, t, 0))
    return pl.pallas_call(
        _fox_prep_kernel,
        grid=(batch, nt),
        in_specs=[
            col_spec(0, WIDTH), col_spec(1, WIDTH), col_spec(2, WIDTH),
            col_spec(4 * wb, LANES),
            pl.BlockSpec((1, LANES), lambda b, t: (0, 0)),
            pl.BlockSpec((1, HEAD_DIM), lambda b, t: (0, 0)),
            pl.BlockSpec((1, HEAD_DIM), lambda b, t: (0, 0)),
        ],
        out_specs=[
            head_spec, head_spec,
            pl.BlockSpec((None, HEADS, tm, 2 * HEAD_DIM), lambda b, t: (b, 0, t, 0)),
            pl.BlockSpec((None, HEADS, tm), lambda b, t: (b, 0, t)),
        ],
        out_shape=[
            jax.ShapeDtypeStruct((batch, HEADS, seq, HEAD_DIM), BF16),
            jax.ShapeDtypeStruct((batch, HEADS, seq, HEAD_DIM), BF16),
            jax.ShapeDtypeStruct((batch, HEADS, seq, 2 * HEAD_DIM), BF16),
            jax.ShapeDtypeStruct((batch, HEADS, seq), F32),
        ],
        scratch_shapes=[pltpu.VMEM((1, LANES), F32)],
        compiler_params=pltpu.CompilerParams(
            dimension_semantics=("parallel", "arbitrary"), vmem_limit_bytes=VMEM_LIMIT_BYTES),
        name="fox_prep",
    )(proj, proj, proj, proj, fb, qw, kw)


def _fox_attn_kernel(q_ref, k_ref, v_ref, nc_ref, z_ref, o_ref, m_ref, acc_ref, *, blk, nsub):
    qi = pl.program_id(1)
    m_ref[...] = jnp.full_like(m_ref, -jnp.inf)
    acc_ref[...] = jnp.zeros_like(acc_ref)

    def steps(work):
        scores = []
        for c, j, masked in work:
            start = pl.multiple_of(j * blk, blk)
            q = q_ref[c * blk:(c + 1) * blk, :]
            k = k_ref[pl.ds(start, blk), :]
            s = lax.dot_general(q, k, NT_DIMS, preferred_element_type=F32)
            s = s + nc_ref[:, pl.ds(start, blk)]
            if masked:
                row = lax.broadcasted_iota(jnp.int32, (blk, blk), 0)
                col = lax.broadcasted_iota(jnp.int32, (blk, blk), 1)
                s = jnp.where(row >= col, s, -jnp.inf)
            scores.append(s)
        for (c, j, _), s in zip(work, scores):
            start = pl.multiple_of(j * blk, blk)
            v = v_ref[pl.ds(start, blk), :]
            m_prev = m_ref[c]
            m_new = jnp.maximum(m_prev, jnp.max(s, axis=-1, keepdims=True))
            alpha = jnp.exp2(m_prev - m_new)
            p = jnp.exp2(s - jnp.tile(m_new, (1, blk // LANES)))
            acc_ref[c] = acc_ref[c] * jnp.tile(alpha, (1, 2)) + jnp.dot(
                p.astype(BF16), v, preferred_element_type=F32)
            m_ref[c] = m_new

    def body(j, carry):
        steps([(c, nsub * j + d, False) for d in range(nsub) for c in range(nsub)])
        return carry

    lax.fori_loop(0, qi, body, 0)
    for d in range(nsub):
        steps([(c, nsub * qi + d, c == d) for c in range(d, nsub)])
    for c in range(nsub):
        acc = acc_ref[c]
        o = acc[:, :HEAD_DIM] / acc[:, HEAD_DIM:HEAD_DIM + 1]
        z = z_ref[c * blk:(c + 1) * blk, :]
        o_ref[c * blk:(c + 1) * blk, :] = o * _silu(z)


def _fox_attn(q, k, v, nc, proj, batch, seq):
    blk = ATTN_BLOCK
    nsub = ATTN_SUBBLOCKS
    tq = blk * nsub
    nq = seq // tq
    bh = batch * HEADS
    z0 = 3 * WIDTH // HEAD_DIM
    return pl.pallas_call(
        functools.partial(_fox_attn_kernel, blk=blk, nsub=nsub),
        grid=(bh, nq),
        in_specs=[
            pl.BlockSpec((None, tq, HEAD_DIM), lambda g, i: (g, i, 0)),
            pl.BlockSpec((None, seq, HEAD_DIM), lambda g, i: (g, 0, 0)),
            pl.BlockSpec((None, seq, 2 * HEAD_DIM), lambda g, i: (g, 0, 0)),
            pl.BlockSpec((None, 1, seq), lambda g, i: (g, 0, 0)),
            pl.BlockSpec((tq, HEAD_DIM), lambda g, i: ((g // HEADS) * nq + i, z0 + g % HEADS)),
        ],
        out_specs=pl.BlockSpec((tq, HEAD_DIM), lambda g, i: ((g // HEADS) * nq + i, g % HEADS)),
        out_shape=jax.ShapeDtypeStruct((batch * seq, WIDTH), F32),
        scratch_shapes=[
            pltpu.VMEM((nsub, blk, LANES), F32),
            pltpu.VMEM((nsub, blk, 2 * HEAD_DIM), F32),
        ],
        compiler_params=pltpu.CompilerParams(
            dimension_semantics=("parallel", "arbitrary"), vmem_limit_bytes=VMEM_LIMIT_BYTES),
        name="fox_attn",
    )(q, k, v, nc, proj)


def _pad_cols(w, ncol):
    return jnp.pad(w, ((0, 0), (0, ncol - w.shape[1])))


def _gdn_in_weights(w_in):
    hg = GDN_HEADS_PER_STEP
    d = w_in.shape[0]
    main = w_in[:, :4 * WIDTH]
    wb = w_in[:, 4 * WIDTH:4 * WIDTH + HEADS]
    wa = w_in[:, 4 * WIDTH + HEADS:]
    blocks = []
    for g in range(HEADS // hg):
        blk = jnp.zeros((d, LANES), w_in.dtype)
        blk = blk.at[:, 0:hg].set(wb[:, g * hg:(g + 1) * hg])
        blk = blk.at[:, SUBLANES:SUBLANES + hg].set(wa[:, g * hg:(g + 1) * hg])
        blocks.append(blk)
    return jnp.concatenate([main] + blocks, axis=1).astype(BF16)


def _gdn_gate_params(a_log, dt_bias):
    hg = GDN_HEADS_PER_STEP
    groups = HEADS // hg
    prm = jnp.zeros((groups, SUBLANES, LANES), F32)
    prm = prm.at[:, 0, SUBLANES:SUBLANES + hg].set(a_log.reshape(groups, hg).astype(F32))
    prm = prm.at[:, 1, SUBLANES:SUBLANES + hg].set(dt_bias.reshape(groups, hg).astype(F32))
    return prm


def _conv_taps(conv_w):
    taps = jnp.pad(conv_w.astype(F32), ((0, SUBLANES - CONV_WIDTH), (0, 0)))
    return taps[:, :WIDTH], taps[:, WIDTH:2 * WIDTH], taps[:, 2 * WIDTH:]


def kernel(x, a_norm_w, a_w_in, a_conv_w, a_A_log, a_dt_bias, a_o_norm_w, a_w_out, b_norm_w, b_w_in, b_f_bias, b_q_norm_w, b_k_norm_w, b_w_out, final_norm_w):
    batch, seq, d_model = x.shape
    depth = a_norm_w.shape[0] + b_norm_w.shape[0]
    h = x.reshape(batch * seq, d_model)
    final_nw = final_norm_w.reshape(1, d_model)
    for i in range(depth):
        j = i // N_MIXERS
        last_nw = final_nw if i == depth - 1 else None
        if i % N_MIXERS == 0:
            proj = _norm_matmul(h, a_norm_w[j].reshape(1, d_model), _gdn_in_weights(a_w_in[j]))
            cw_q, cw_k, cw_v = _conv_taps(a_conv_w[j])
            y = _gdn(proj, cw_q, cw_k, cw_v, _gdn_gate_params(a_A_log[j], a_dt_bias[j]),
                     a_o_norm_w[j].reshape(1, HEAD_DIM), batch, seq)
            h = _proj_residual(y, h, a_w_out[j].astype(BF16), last_nw)
        else:
            ncol = 4 * WIDTH + LANES
            proj = _norm_matmul(h, b_norm_w[j].reshape(1, d_model),
                                _pad_cols(b_w_in[j], ncol).astype(BF16))
            fb = jnp.pad(b_f_bias[j].astype(F32), (0, LANES - HEADS)).reshape(1, LANES)
            q, k, v, nc = _fox_prep(proj, fb, b_q_norm_w[j].reshape(1, HEAD_DIM),
                                    b_k_norm_w[j].reshape(1, HEAD_DIM), batch, seq)
            bh = batch * HEADS
            y = _fox_attn(q.reshape(bh, seq, HEAD_DIM), k.reshape(bh, seq, HEAD_DIM),
                          v.reshape(bh, seq, 2 * HEAD_DIM), nc.reshape(bh, 1, seq),
                          proj, batch, seq)
            h = _proj_residual(y, h, b_w_out[j].astype(BF16), last_nw)
    return h.reshape(batch, seq, d_model)
```

```python
import functools

import jax
import jax.numpy as jnp
from jax import lax
from jax.experimental import pallas as pl
from jax.experimental.pallas import tpu as pltpu

F32 = jnp.float32
BF16 = jnp.bfloat16
EPS = 1e-6

HEADS = 8
HEAD_DIM = 128
WIDTH = HEADS * HEAD_DIM
CONV_WIDTH = 4
N_MIXERS = 2

LANES = 128
SUBLANES = 8
MXU_COLS = 256
VMEM_LIMIT_BYTES = 56 * 1024 * 1024

GDN_CHUNK = 256
GDN_BASE_BLOCK = 16
PROJ_ROWS = 512
ATTN_BLOCK = 512
ATTN_SUBBLOCKS = 2
LOG2E = 1.4426950408889634

NT_DIMS = (((1,), (1,)), ((), ()))


def _sigmoid(x):
    return 0.5 + 0.5 * jnp.tanh(0.5 * x)


def _silu(x):
    h = 0.5 * x
    return h + h * jnp.tanh(h)


def _softplus(x):
    return jnp.maximum(x, 0.0) + jnp.log1p(jnp.exp(-jnp.abs(x)))


def _mm(a, b):
    return jnp.dot(a.astype(BF16), b.astype(BF16), preferred_element_type=F32)


def _mm_nt(a, b):
    return lax.dot_general(a.astype(BF16), b.astype(BF16), NT_DIMS, preferred_element_type=F32)


def _dot_b(a, b):
    return jnp.dot(a, b, preferred_element_type=F32).astype(BF16)


def _normed_rows(x_ref, nw_ref):
    x = x_ref[...]
    ms = jnp.mean(x * x, axis=-1, keepdims=True)
    return (x * lax.rsqrt(ms + EPS) * nw_ref[...]).astype(BF16)


def _lane_cumsum(x):
    n = x.shape[-1]
    lane = lax.broadcasted_iota(jnp.int32, x.shape, x.ndim - 1)
    shift = 1
    while shift < n:
        x = x + jnp.where(lane >= shift, pltpu.roll(x, shift, x.ndim - 1), 0.0)
        shift *= 2
    return x


def _proj_gdn_kernel(x_ref, nw_ref, w_ref, cw_ref, qkv_ref, z_ref, ba_ref, tail_ref):
    tm = x_ref.shape[0]
    cw = MXU_COLS
    last = CONV_WIDTH - 1

    @pl.when(pl.program_id(1) == 0)
    def _():
        tail_ref[...] = jnp.zeros_like(tail_ref)

    hn = _normed_rows(x_ref, nw_ref)
    row8 = lax.broadcasted_iota(jnp.int32, (SUBLANES, cw), 0)
    for c in range(3 * WIDTH // cw):
        cols = slice(c * cw, (c + 1) * cw)
        pre = jnp.dot(hn, w_ref[:, cols], preferred_element_type=F32)
        tail = tail_ref[c]
        taps = cw_ref[:, cols]
        acc = pre * taps[last:last + 1, :]
        for s in range(1, CONV_WIDTH):
            xs = pltpu.roll(pre, s, 0)
            head = jnp.where(row8 < s, pltpu.roll(tail, s, 0), xs[0:SUBLANES, :])
            xs = jnp.concatenate([head, xs[SUBLANES:tm, :]], axis=0)
            acc = acc + xs * taps[last - s:last - s + 1, :]
        tail_ref[c] = pre[tm - SUBLANES:tm, :]
        act = _silu(acc)
        if c < 2 * WIDTH // cw:
            scale = HEAD_DIM ** -0.5 if c < WIDTH // cw else 1.0
            parts = []
            for h in range(cw // HEAD_DIM):
                a = act[:, h * HEAD_DIM:(h + 1) * HEAD_DIM]
                r = lax.rsqrt(jnp.sum(a * a, axis=-1, keepdims=True) + EPS)
                parts.append(a * (r * scale))
            act = jnp.concatenate(parts, axis=1)
        qkv_ref[:, cols] = act.astype(BF16)
    for c in range(WIDTH // cw):
        cols = slice(3 * WIDTH + c * cw, 3 * WIDTH + (c + 1) * cw)
        z_ref[:, c * cw:(c + 1) * cw] = jnp.dot(hn, w_ref[:, cols], preferred_element_type=F32)
    ba_ref[...] = jnp.dot(hn, w_ref[:, 4 * WIDTH:4 * WIDTH + LANES], preferred_element_type=F32)


def _proj_gdn(x, nw, w, cw, batch, seq):
    n, d = x.shape
    ncol = w.shape[1]
    tm = PROJ_ROWS
    nt = seq // tm

    def row_spec(width):
        return pl.BlockSpec((tm, width), lambda b, t: (b * nt + t, 0))

    return pl.pallas_call(
        _proj_gdn_kernel,
        grid=(batch, nt),
        in_specs=[
            row_spec(d),
            pl.BlockSpec((1, d), lambda b, t: (0, 0)),
            pl.BlockSpec((d, ncol), lambda b, t: (0, 0)),
            pl.BlockSpec((SUBLANES, 3 * WIDTH), lambda b, t: (0, 0)),
        ],
        out_specs=[row_spec(3 * WIDTH), row_spec(WIDTH), row_spec(LANES)],
        out_shape=[
            jax.ShapeDtypeStruct((n, 3 * WIDTH), BF16),
            jax.ShapeDtypeStruct((n, WIDTH), F32),
            jax.ShapeDtypeStruct((n, LANES), F32),
        ],
        scratch_shapes=[pltpu.VMEM((3 * WIDTH // MXU_COLS, SUBLANES, MXU_COLS), F32)],
        compiler_params=pltpu.CompilerParams(
            dimension_semantics=("parallel", "arbitrary"), vmem_limit_bytes=VMEM_LIMIT_BYTES),
        name="proj_gdn",
    )(x, nw, w, cw)


def _proj_residual_kernel(y_ref, r_ref, w_ref, o_ref):
    o_ref[...] = r_ref[...] + jnp.dot(y_ref[...].astype(BF16), w_ref[...],
                                      preferred_element_type=F32)


def _proj_residual_norm_kernel(y_ref, r_ref, w_ref, nw_ref, o_ref):
    h = r_ref[...] + jnp.dot(y_ref[...].astype(BF16), w_ref[...], preferred_element_type=F32)
    ms = jnp.mean(h * h, axis=-1, keepdims=True)
    o_ref[...] = h * lax.rsqrt(ms + EPS) * nw_ref[...]


def _proj_residual(y, res, w, final_nw=None):
    n, d_in = y.shape
    d_out = w.shape[1]
    tm = PROJ_ROWS
    in_specs = [
        pl.BlockSpec((tm, d_in), lambda i: (i, 0)),
        pl.BlockSpec((tm, d_out), lambda i: (i, 0)),
        pl.BlockSpec((d_in, d_out), lambda i: (0, 0)),
    ]
    args = [y, res, w]
    body = _proj_residual_kernel
    if final_nw is not None:
        in_specs.append(pl.BlockSpec((1, d_out), lambda i: (0, 0)))
        args.append(final_nw)
        body = _proj_residual_norm_kernel
    return pl.pallas_call(
        body,
        grid=(n // tm,),
        in_specs=in_specs,
        out_specs=pl.BlockSpec((tm, d_out), lambda i: (i, 0)),
        out_shape=jax.ShapeDtypeStruct((n, d_out), F32),
        compiler_params=pltpu.CompilerParams(
            dimension_semantics=("parallel",), vmem_limit_bytes=VMEM_LIMIT_BYTES),
        name="proj_residual",
    )(*args)


def _block_masks(c):
    row = lax.broadcasted_iota(jnp.int32, (c, c), 0)
    col = lax.broadcasted_iota(jnp.int32, (c, c), 1)
    xr = row ^ col
    eye = jnp.where(row == col, 1.0, 0.0).astype(BF16)
    b = GDN_BASE_BLOCK
    below = jnp.where(xr < b, 1.0, 0.0)
    masks = [below.astype(BF16)]
    while b < c:
        below_next = jnp.where(xr < 2 * b, 1.0, 0.0)
        masks.append((below_next - below).astype(BF16))
        below = below_next
        b *= 2
    return eye, masks


def _unit_lower_inverses(a_list, eye, masks):
    c = eye.shape[0]
    ps = [a * masks[0] for a in a_list]
    xs = [eye - p for p in ps]
    width = 2
    while width < GDN_BASE_BLOCK:
        ps = [_dot_b(p, p) for p in ps]
        xs = [x + _dot_b(x, p) for x, p in zip(xs, ps)]
        width *= 2
    b = GDN_BASE_BLOCK
    for m in masks[1:]:
        starts = range(0, c, 2 * b)
        lows = [jnp.concatenate([x[s + b:s + 2 * b, :] for s in starts], axis=0) for x in xs]
        ys = [_dot_b(lo, a * m) for lo, a in zip(lows, a_list)]
        lows = [lo - _dot_b(y, x) for lo, y, x in zip(lows, ys, xs)]
        xs = [jnp.concatenate([piece for k, s in enumerate(starts)
                               for piece in (x[s:s + b, :], lo[k * b:(k + 1) * b, :])], axis=0)
              for x, lo in zip(xs, lows)]
        b *= 2
    return xs


def _gdn_kernel(q_ref, k_ref, v_ref, z_ref, ba_ref, prm_ref, onw_ref, y_ref, s_ref):
    c = q_ref.shape[0]

    @pl.when(pl.program_id(1) == 0)
    def _():
        s_ref[...] = jnp.zeros_like(s_ref)

    row = lax.broadcasted_iota(jnp.int32, (c, c), 0)
    col = lax.broadcasted_iota(jnp.int32, (c, c), 1)
    lower_incl = row >= col
    lower_strict = row > col
    eye, masks = _block_masks(c)

    ba = ba_ref[...]
    neg_a = -jnp.exp(prm_ref[0:1, :])
    beta_all = _sigmoid(ba)
    glog = neg_a * _softplus(ba + prm_ref[1:2, :])
    gcum_t = _lane_cumsum(glog.T[0:2 * HEADS, :])
    gcum = jnp.concatenate([gcum_t, jnp.zeros((LANES - 2 * HEADS, c), F32)], axis=0).T

    heads = range(HEADS)
    sls = [slice(i * HEAD_DIM, (i + 1) * HEAD_DIM) for i in heads]
    gcols = [gcum[:, HEADS + i:HEADS + i + 1] for i in heads]
    glasts = [gcum[c - 1:c, HEADS + i:HEADS + i + 1] for i in heads]
    vb, kbg, qd, kend, a_l, attn = [], [], [], [], [], []
    for i in heads:
        qn_b = q_ref[:, sls[i]]
        kn_b = k_ref[:, sls[i]]
        qn = qn_b.astype(F32)
        kn = kn_b.astype(F32)
        beta = beta_all[:, i:i + 1]
        eg = jnp.exp(gcols[i])
        kb = kn * beta
        vb.append(v_ref[:, sls[i]].astype(F32) * beta)
        kbg.append(kb * eg)
        qd.append(qn * eg)
        kend.append(kn * jnp.exp(glasts[i] - gcols[i]))
        grow = gcum_t[HEADS + i:HEADS + i + 1, :]
        dec = jnp.exp(jnp.where(lower_incl, gcols[i] - grow, -jnp.inf))
        a_l.append(jnp.where(lower_strict, _mm_nt(kb, kn_b) * dec, 0.0).astype(BF16))
        attn.append((_mm_nt(qn_b, kn_b) * dec).astype(BF16))
    ns = [x - eye for x in _unit_lower_inverses(a_l, eye, masks)]
    uws = [jnp.dot(ns[i], jnp.concatenate([vb[i], kbg[i]], axis=1).astype(BF16),
                   preferred_element_type=F32) for i in heads]
    us = [vb[i] + uws[i][:, :HEAD_DIM] for i in heads]
    wq = [jnp.concatenate([kbg[i] + uws[i][:, HEAD_DIM:], qd[i]], axis=0).astype(BF16)
          for i in heads]
    ss = [s_ref[i] for i in heads]
    s_bs = [s.astype(BF16) for s in ss]
    zero_s = jnp.zeros((HEAD_DIM, HEAD_DIM), BF16)
    wss = []
    for i in range(0, HEADS, 2):
        s_pair = jnp.concatenate([jnp.concatenate([s_bs[i], zero_s], axis=1),
                                  jnp.concatenate([zero_s, s_bs[i + 1]], axis=1)], axis=0)
        both = jnp.dot(jnp.concatenate([wq[i], wq[i + 1]], axis=1), s_pair,
                       preferred_element_type=F32)
        wss += [both[:, :HEAD_DIM], both[:, HEAD_DIM:]]
    v_new_b = [(us[i] - wss[i][:c, :]).astype(BF16) for i in heads]
    os_ = [wss[i][c:, :] + jnp.dot(attn[i], v_new_b[i], preferred_element_type=F32)
           for i in heads]
    for i in heads:
        s_ref[i] = ss[i] * jnp.exp(glasts[i]) + _mm(kend[i].T, v_new_b[i])
    for i in heads:
        o = os_[i]
        on = o * lax.rsqrt(jnp.mean(o * o, axis=-1, keepdims=True) + EPS) * onw_ref[...]
        y_ref[:, sls[i]] = on * _silu(z_ref[:, sls[i]])


def _gdn(qkv, z, ba, prm, onw, batch, seq):
    n = qkv.shape[0]
    c = GDN_CHUNK
    nt = seq // c

    def row_spec(width, col):
        return pl.BlockSpec((c, width), lambda b, t: (b * nt + t, col))

    return pl.pallas_call(
        _gdn_kernel,
        grid=(batch, nt),
        in_specs=[
            row_spec(WIDTH, 0), row_spec(WIDTH, 1), row_spec(WIDTH, 2),
            row_spec(WIDTH, 0),
            row_spec(LANES, 0),
            pl.BlockSpec((SUBLANES, LANES), lambda b, t: (0, 0)),
            pl.BlockSpec((1, HEAD_DIM), lambda b, t: (0, 0)),
        ],
        out_specs=row_spec(WIDTH, 0),
        out_shape=jax.ShapeDtypeStruct((n, WIDTH), F32),
        scratch_shapes=[pltpu.VMEM((HEADS, HEAD_DIM, HEAD_DIM), F32)],
        compiler_params=pltpu.CompilerParams(
            dimension_semantics=("parallel", "arbitrary"), vmem_limit_bytes=VMEM_LIMIT_BYTES),
        name="gdn",
    )(qkv, qkv, qkv, z, ba, prm, onw)


def _proj_fox_kernel(x_ref, nw_ref, w_ref, fb_ref, qw_ref, kw_ref,
                     qo_ref, ko_ref, vo_ref, z_ref, nc_ref, carry_ref):
    tm = x_ref.shape[0]
    cw = MXU_COLS
    per = cw // HEAD_DIM

    @pl.when(pl.program_id(1) == 0)
    def _():
        carry_ref[...] = jnp.zeros_like(carry_ref)

    hn = _normed_rows(x_ref, nw_ref)

    def chunk(col0):
        return jnp.dot(hn, w_ref[:, col0:col0 + cw], preferred_element_type=F32)

    q_gain = qw_ref[...] * (HEAD_DIM ** -0.5 * LOG2E)
    k_gain = kw_ref[...]
    for c in range(WIDTH // cw):
        res = chunk(c * cw)
        for h in range(per):
            a = res[:, h * HEAD_DIM:(h + 1) * HEAD_DIM]
            r = lax.rsqrt(jnp.mean(a * a, axis=-1, keepdims=True) + EPS)
            qo_ref[c * per + h] = (a * r * q_gain).astype(BF16)
    for c in range(WIDTH // cw):
        res = chunk(WIDTH + c * cw)
        for h in range(per):
            a = res[:, h * HEAD_DIM:(h + 1) * HEAD_DIM]
            r = lax.rsqrt(jnp.mean(a * a, axis=-1, keepdims=True) + EPS)
            ko_ref[c * per + h] = (a * r * k_gain).astype(BF16)
    lane = lax.broadcasted_iota(jnp.int32, (tm, LANES), 1)
    ones_col = jnp.where(lane == 0, 1.0, 0.0).astype(BF16)
    for c in range(WIDTH // cw):
        res = chunk(2 * WIDTH + c * cw)
        for h in range(per):
            vo_ref[c * per + h, :, 0:HEAD_DIM] = res[:, h * HEAD_DIM:(h + 1) * HEAD_DIM].astype(BF16)
            vo_ref[c * per + h, :, HEAD_DIM:2 * HEAD_DIM] = ones_col
    for c in range(WIDTH // cw):
        z_ref[:, c * cw:(c + 1) * cw] = chunk(3 * WIDTH + c * cw)

    f = jnp.dot(hn, w_ref[:, 4 * WIDTH:4 * WIDTH + LANES], preferred_element_type=F32) + fb_ref[...]
    logf = jnp.minimum(f, 0.0) - jnp.log1p(jnp.exp(-jnp.abs(f)))
    cum = _lane_cumsum(logf.T[0:HEADS, :]) + carry_ref[...]
    carry_ref[...] = cum[:, tm - 1:tm]
    nc_ref[...] = cum * (-LOG2E)


def _proj_fox(x, nw, w, fb, qw, kw, batch, seq):
    n, d = x.shape
    ncol = w.shape[1]
    tm = PROJ_ROWS
    nt = seq // tm
    head_spec = pl.BlockSpec((None, HEADS, tm, HEAD_DIM), lambda b, t: (b, 0, t, 0))
    return pl.pallas_call(
        _proj_fox_kernel,
        grid=(batch, nt),
        in_specs=[
            pl.BlockSpec((tm, d), lambda b, t: (b * nt + t, 0)),
            pl.BlockSpec((1, d), lambda b, t: (0, 0)),
            pl.BlockSpec((d, ncol), lambda b, t: (0, 0)),
            pl.BlockSpec((1, LANES), lambda b, t: (0, 0)),
            pl.BlockSpec((1, HEAD_DIM), lambda b, t: (0, 0)),
            pl.BlockSpec((1, HEAD_DIM), lambda b, t: (0, 0)),
        ],
        out_specs=[
            head_spec, head_spec,
            pl.BlockSpec((None, HEADS, tm, 2 * HEAD_DIM), lambda b, t: (b, 0, t, 0)),
            pl.BlockSpec((tm, WIDTH), lambda b, t: (b * nt + t, 0)),
            pl.BlockSpec((None, HEADS, tm), lambda b, t: (b, 0, t)),
        ],
        out_shape=[
            jax.ShapeDtypeStruct((batch, HEADS, seq, HEAD_DIM), BF16),
            jax.ShapeDtypeStruct((batch, HEADS, seq, HEAD_DIM), BF16),
            jax.ShapeDtypeStruct((batch, HEADS, seq, 2 * HEAD_DIM), BF16),
            jax.ShapeDtypeStruct((n, WIDTH), F32),
            jax.ShapeDtypeStruct((batch, HEADS, seq), F32),
        ],
        scratch_shapes=[pltpu.VMEM((HEADS, 1), F32)],
        compiler_params=pltpu.CompilerParams(
            dimension_semantics=("parallel", "arbitrary"), vmem_limit_bytes=VMEM_LIMIT_BYTES),
        name="proj_fox",
    )(x, nw, w, fb, qw, kw)


def _fox_attn_kernel(q_ref, k_ref, v_ref, nc_ref, z_ref, o_ref, m_ref, acc_ref, s_ref, *, blk, nsub):
    qi = pl.program_id(1)
    m_ref[...] = jnp.full_like(m_ref, -jnp.inf)
    acc_ref[...] = jnp.zeros_like(acc_ref)
    pairs = [(d, c) for d in range(nsub) for c in range(nsub)]

    def score(g, w):
        d, c = pairs[w]
        start = pl.multiple_of((nsub * g + d) * blk, blk)
        q = q_ref[c * blk:(c + 1) * blk, :]
        k = k_ref[pl.ds(start, blk), :]
        s_ref[w] = (lax.dot_general(q, k, NT_DIMS, preferred_element_type=F32)
                    + nc_ref[:, pl.ds(start, blk)])

    def update(g, w, masked):
        d, c = pairs[w]
        start = pl.multiple_of((nsub * g + d) * blk, blk)
        s = s_ref[w]
        if masked:
            row = lax.broadcasted_iota(jnp.int32, (blk, blk), 0)
            col = lax.broadcasted_iota(jnp.int32, (blk, blk), 1)
            s = jnp.where(row >= col, s, -jnp.inf)
        v = v_ref[pl.ds(start, blk), :]
        m_prev = m_ref[c]
        m_new = jnp.maximum(m_prev, jnp.max(s, axis=-1, keepdims=True))
        alpha = jnp.exp2(m_prev - m_new)
        p = jnp.exp2(s - jnp.tile(m_new, (1, blk // LANES)))
        acc_ref[c] = acc_ref[c] * jnp.tile(alpha, (1, 2)) + jnp.dot(
            p.astype(BF16), v, preferred_element_type=F32)
        m_ref[c] = m_new

    for w in range(len(pairs)):
        score(0, w)

    def body(g, carry):
        for w in range(len(pairs)):
            update(g, w, False)
            score(g + 1, w)
        return carry

    lax.fori_loop(0, qi, body, 0)
    for w, (d, c) in enumerate(pairs):
        if c >= d:
            update(qi, w, c == d)
    for c in range(nsub):
        acc = acc_ref[c]
        o = acc[:, :HEAD_DIM] / acc[:, HEAD_DIM:HEAD_DIM + 1]
        z = z_ref[c * blk:(c + 1) * blk, :]
        o_ref[c * blk:(c + 1) * blk, :] = o * _silu(z)


def _fox_attn(q, k, v, nc, z, batch, seq):
    blk = ATTN_BLOCK
    nsub = ATTN_SUBBLOCKS
    tq = blk * nsub
    nq = seq // tq
    bh = batch * HEADS
    out_spec = pl.BlockSpec((tq, HEAD_DIM), lambda g, i: ((g // HEADS) * nq + i, g % HEADS))
    return pl.pallas_call(
        functools.partial(_fox_attn_kernel, blk=blk, nsub=nsub),
        grid=(bh, nq),
        in_specs=[
            pl.BlockSpec((None, tq, HEAD_DIM), lambda g, i: (g, i, 0)),
            pl.BlockSpec((None, seq, HEAD_DIM), lambda g, i: (g, 0, 0)),
            pl.BlockSpec((None, seq, 2 * HEAD_DIM), lambda g, i: (g, 0, 0)),
            pl.BlockSpec((None, 1, seq), lambda g, i: (g, 0, 0)),
            out_spec,
        ],
        out_specs=out_spec,
        out_shape=jax.ShapeDtypeStruct((batch * seq, WIDTH), F32),
        scratch_shapes=[
            pltpu.VMEM((nsub, blk, LANES), F32),
            pltpu.VMEM((nsub, blk, 2 * HEAD_DIM), F32),
            pltpu.VMEM((nsub * nsub, blk, blk), F32),
        ],
        compiler_params=pltpu.CompilerParams(
            dimension_semantics=("parallel", "arbitrary"), vmem_limit_bytes=VMEM_LIMIT_BYTES),
        name="fox_attn",
    )(q, k, v, nc, z)


def _pad_cols(w, ncol):
    return jnp.pad(w, ((0, 0), (0, ncol - w.shape[1])))


def _gdn_gate_params(a_log, dt_bias):
    prm = jnp.zeros((SUBLANES, LANES), F32)
    prm = prm.at[0, HEADS:2 * HEADS].set(a_log.astype(F32))
    prm = prm.at[1, HEADS:2 * HEADS].set(dt_bias.astype(F32))
    return prm


def kernel(x, a_norm_w, a_w_in, a_conv_w, a_A_log, a_dt_bias, a_o_norm_w, a_w_out, b_norm_w, b_w_in, b_f_bias, b_q_norm_w, b_k_norm_w, b_w_out, final_norm_w):
    batch, seq, d_model = x.shape
    depth = a_norm_w.shape[0] + b_norm_w.shape[0]
    ncol = 4 * WIDTH + LANES
    h = x.reshape(batch * seq, d_model)
    final_nw = final_norm_w.reshape(1, d_model)
    for i in range(depth):
        j = i // N_MIXERS
        last_nw = final_nw if i == depth - 1 else None
        if i % N_MIXERS == 0:
            taps = jnp.pad(a_conv_w[j].astype(F32), ((0, SUBLANES - CONV_WIDTH), (0, 0)))
            qkv, z, ba = _proj_gdn(h, a_norm_w[j].reshape(1, d_model),
                                   _pad_cols(a_w_in[j], ncol).astype(BF16), taps, batch, seq)
            y = _gdn(qkv, z, ba, _gdn_gate_params(a_A_log[j], a_dt_bias[j]),
                     a_o_norm_w[j].reshape(1, HEAD_DIM), batch, seq)
            h = _proj_residual(y, h, a_w_out[j].astype(BF16), last_nw)
        else:
            fb = jnp.pad(b_f_bias[j].astype(F32), (0, LANES - HEADS)).reshape(1, LANES)
            q, k, v, z, nc = _proj_fox(h, b_norm_w[j].reshape(1, d_model),
                                       _pad_cols(b_w_in[j], ncol).astype(BF16), fb,
                                       b_q_norm_w[j].reshape(1, HEAD_DIM),
                                       b_k_norm_w[j].reshape(1, HEAD_DIM), batch, seq)
            bh = batch * HEADS
            y = _fox_attn(q.reshape(bh, seq, HEAD_DIM), k.reshape(bh, seq, HEAD_DIM),
                          v.reshape(bh, seq, 2 * HEAD_DIM), nc.reshape(bh, 1, seq),
                          z, batch, seq)
            h = _proj_residual(y, h, b_w_out[j].astype(BF16), last_nw)
    return h.reshape(batch, seq, d_model)
```

```python
import functools

import jax
import jax.numpy as jnp
from jax import lax
from jax.experimental import pallas as pl
from jax.experimental.pallas import tpu as pltpu

F32 = jnp.float32
BF16 = jnp.bfloat16
EPS = 1e-6

HEADS = 8
HEAD_DIM = 128
WIDTH = HEADS * HEAD_DIM
CONV_WIDTH = 4
N_MIXERS = 2

LANES = 128
SUBLANES = 8
MXU_COLS = 256
VMEM_LIMIT_BYTES = 56 * 1024 * 1024

GDN_CHUNK = 256
GDN_BASE_BLOCK = 16
PROJ_ROWS = 512
ATTN_BLOCK = 512
ATTN_SUBBLOCKS = 2
LOG2E = 1.4426950408889634

NT_DIMS = (((1,), (1,)), ((), ()))


def _sigmoid(x):
    return 0.5 + 0.5 * jnp.tanh(0.5 * x)


def _silu(x):
    h = 0.5 * x
    return h + h * jnp.tanh(h)


def _softplus(x):
    return jnp.maximum(x, 0.0) + jnp.log1p(jnp.exp(-jnp.abs(x)))


def _mm(a, b):
    return jnp.dot(a.astype(BF16), b.astype(BF16), preferred_element_type=F32)


def _mm_nt(a, b):
    return lax.dot_general(a.astype(BF16), b.astype(BF16), NT_DIMS, preferred_element_type=F32)


def _dot_b(a, b):
    return jnp.dot(a, b, preferred_element_type=F32).astype(BF16)


def _normed_rows(x_ref, nw_ref):
    x = x_ref[...]
    ms = jnp.mean(x * x, axis=-1, keepdims=True)
    return (x * lax.rsqrt(ms + EPS) * nw_ref[...]).astype(BF16)


def _lane_cumsum(x):
    n = x.shape[-1]
    lane = lax.broadcasted_iota(jnp.int32, x.shape, x.ndim - 1)
    shift = 1
    while shift < n:
        x = x + jnp.where(lane >= shift, pltpu.roll(x, shift, x.ndim - 1), 0.0)
        shift *= 2
    return x


def _conv_silu_slab(xs_ref, ys_ref, k, taps, tm):
    nph = (tm + SUBLANES) // SUBLANES
    last = CONV_WIDTH - 1
    taps = [jnp.broadcast_to(taps[s:s + 1, :], (SUBLANES, LANES)) for s in range(CONV_WIDTH)]
    ph = [xs_ref[k, pl.ds(r, SUBLANES, stride=nph), :] for r in range(nph)]
    wrapped = {r: pltpu.roll(ph[r], 1, 0) for r in range(nph - last, nph)}
    for r in range(nph):
        acc = ph[r] * taps[last]
        for s in range(1, CONV_WIDTH):
            src = ph[r - s] if r >= s else wrapped[r - s + nph]
            acc = acc + src * taps[last - s]
        ys_ref[k, pl.ds(r, SUBLANES, stride=nph), :] = _silu(acc)


def _proj_gdn_kernel(x_ref, nw_ref, w_ref, cw_ref, qkv_ref, z_ref, ba_ref, tail_ref, xs_ref, ys_ref):
    tm = x_ref.shape[0]
    cw = MXU_COLS
    per = cw // LANES
    assert (tm + SUBLANES) // SUBLANES % 2 == 1

    @pl.when(pl.program_id(1) == 0)
    def _():
        tail_ref[...] = jnp.zeros_like(tail_ref)

    hn = _normed_rows(x_ref, nw_ref)
    n_conv = 3 * WIDTH // cw

    def chunk_dot(c):
        return jnp.dot(hn, w_ref[:, c * cw:(c + 1) * cw], preferred_element_type=F32)

    pre_next = chunk_dot(0)
    for c in range(n_conv):
        pre = pre_next
        if c + 1 < n_conv:
            pre_next = chunk_dot(c + 1)
        for h in range(per):
            slab = c * per + h
            k = (c % 2) * per + h
            lanes = slice(slab * LANES, (slab + 1) * LANES)
            pre_h = pre[:, h * LANES:(h + 1) * LANES]
            xs_ref[k, 0:SUBLANES, :] = tail_ref[slab]
            xs_ref[k, SUBLANES:SUBLANES + tm, :] = pre_h
            tail_ref[slab] = pre_h[tm - SUBLANES:tm, :]
            _conv_silu_slab(xs_ref, ys_ref, k, cw_ref[:, lanes], tm)
            act = ys_ref[k, SUBLANES:SUBLANES + tm, :]
            if slab < 2 * HEADS:
                scale = HEAD_DIM ** -0.5 if slab < HEADS else 1.0
                r = lax.rsqrt(jnp.sum(act * act, axis=-1, keepdims=True) + EPS)
                act = act * (r * scale)
            qkv_ref[:, lanes] = act.astype(BF16)
    for c in range(WIDTH // cw):
        cols = slice(3 * WIDTH + c * cw, 3 * WIDTH + (c + 1) * cw)
        z_ref[:, c * cw:(c + 1) * cw] = jnp.dot(hn, w_ref[:, cols], preferred_element_type=F32)
    ba_ref[...] = jnp.dot(hn, w_ref[:, 4 * WIDTH:4 * WIDTH + LANES], preferred_element_type=F32)


def _proj_gdn(x, nw, w, cw, batch, seq):
    n, d = x.shape
    ncol = w.shape[1]
    tm = PROJ_ROWS
    nt = seq // tm

    def row_spec(width):
        return pl.BlockSpec((tm, width), lambda b, t: (b * nt + t, 0))

    return pl.pallas_call(
        _proj_gdn_kernel,
        grid=(batch, nt),
        in_specs=[
            row_spec(d),
            pl.BlockSpec((1, d), lambda b, t: (0, 0)),
            pl.BlockSpec((d, ncol), lambda b, t: (0, 0)),
            pl.BlockSpec((SUBLANES, 3 * WIDTH), lambda b, t: (0, 0)),
        ],
        out_specs=[row_spec(3 * WIDTH), row_spec(WIDTH), row_spec(LANES)],
        out_shape=[
            jax.ShapeDtypeStruct((n, 3 * WIDTH), BF16),
            jax.ShapeDtypeStruct((n, WIDTH), F32),
            jax.ShapeDtypeStruct((n, LANES), F32),
        ],
        scratch_shapes=[
            pltpu.VMEM((3 * WIDTH // LANES, SUBLANES, LANES), F32),
            pltpu.VMEM((2 * MXU_COLS // LANES, tm + SUBLANES, LANES), F32),
            pltpu.VMEM((2 * MXU_COLS // LANES, tm + SUBLANES, LANES), F32),
        ],
        compiler_params=pltpu.CompilerParams(
            dimension_semantics=("parallel", "arbitrary"), vmem_limit_bytes=VMEM_LIMIT_BYTES),
        name="proj_gdn",
    )(x, nw, w, cw)


def _proj_residual_kernel(y_ref, r_ref, w_ref, o_ref):
    o_ref[...] = r_ref[...] + jnp.dot(y_ref[...], w_ref[...], preferred_element_type=F32)


def _proj_residual_norm_kernel(y_ref, r_ref, w_ref, nw_ref, o_ref):
    h = r_ref[...] + jnp.dot(y_ref[...], w_ref[...], preferred_element_type=F32)
    ms = jnp.mean(h * h, axis=-1, keepdims=True)
    o_ref[...] = h * lax.rsqrt(ms + EPS) * nw_ref[...]


def _proj_residual(y, res, w, final_nw=None):
    n, d_in = y.shape
    d_out = w.shape[1]
    tm = PROJ_ROWS
    in_specs = [
        pl.BlockSpec((tm, d_in), lambda i: (i, 0)),
        pl.BlockSpec((tm, d_out), lambda i: (i, 0)),
        pl.BlockSpec((d_in, d_out), lambda i: (0, 0)),
    ]
    args = [y, res, w]
    body = _proj_residual_kernel
    if final_nw is not None:
        in_specs.append(pl.BlockSpec((1, d_out), lambda i: (0, 0)))
        args.append(final_nw)
        body = _proj_residual_norm_kernel
    return pl.pallas_call(
        body,
        grid=(n // tm,),
        in_specs=in_specs,
        out_specs=pl.BlockSpec((tm, d_out), lambda i: (i, 0)),
        out_shape=jax.ShapeDtypeStruct((n, d_out), F32),
        compiler_params=pltpu.CompilerParams(
            dimension_semantics=("parallel",), vmem_limit_bytes=VMEM_LIMIT_BYTES),
        name="proj_residual",
    )(*args)


def _block_masks(c):
    row = lax.broadcasted_iota(jnp.int32, (c, c), 0)
    col = lax.broadcasted_iota(jnp.int32, (c, c), 1)
    xr = row ^ col
    eye = jnp.where(row == col, 1.0, 0.0).astype(BF16)
    b = GDN_BASE_BLOCK
    below = jnp.where(xr < b, 1.0, 0.0)
    masks = [below.astype(BF16)]
    while b < c:
        below_next = jnp.where(xr < 2 * b, 1.0, 0.0)
        masks.append((below_next - below).astype(BF16))
        below = below_next
        b *= 2
    return eye, masks


def _unit_lower_inverses(a_list, eye, masks):
    c = eye.shape[0]
    ps = [a * masks[0] for a in a_list]
    xs = [eye - p for p in ps]
    width = 2
    while width < GDN_BASE_BLOCK:
        ps = [_dot_b(p, p) for p in ps]
        xs = [x + _dot_b(x, p) for x, p in zip(xs, ps)]
        width *= 2
    b = GDN_BASE_BLOCK
    for m in masks[1:]:
        starts = range(0, c, 2 * b)
        lows = [jnp.concatenate([x[s + b:s + 2 * b, :] for s in starts], axis=0) for x in xs]
        ys = [_dot_b(lo, a * m) for lo, a in zip(lows, a_list)]
        lows = [lo - _dot_b(y, x) for lo, y, x in zip(lows, ys, xs)]
        xs = [jnp.concatenate([piece for k, s in enumerate(starts)
                               for piece in (x[s:s + b, :], lo[k * b:(k + 1) * b, :])], axis=0)
              for x, lo in zip(xs, lows)]
        b *= 2
    return xs


def _gdn_kernel(q_ref, k_ref, v_ref, z_ref, ba_ref, prm_ref, onw_ref, y_ref, s_ref):
    c = q_ref.shape[0]

    @pl.when(pl.program_id(1) == 0)
    def _():
        s_ref[...] = jnp.zeros_like(s_ref)

    row = lax.broadcasted_iota(jnp.int32, (c, c), 0)
    col = lax.broadcasted_iota(jnp.int32, (c, c), 1)
    lower_incl = row >= col
    lower_strict = row > col
    eye, masks = _block_masks(c)

    ba = ba_ref[...]
    neg_a = -jnp.exp(prm_ref[0:1, :])
    beta_all = _sigmoid(ba)
    glog = neg_a * _softplus(ba + prm_ref[1:2, :])
    gcum_t = _lane_cumsum(glog.T[0:2 * HEADS, :])
    gcum = jnp.concatenate([gcum_t, jnp.zeros((LANES - 2 * HEADS, c), F32)], axis=0).T

    heads = range(HEADS)
    sls = [slice(i * HEAD_DIM, (i + 1) * HEAD_DIM) for i in heads]
    gcols = [gcum[:, HEADS + i:HEADS + i + 1] for i in heads]
    glasts = [gcum[c - 1:c, HEADS + i:HEADS + i + 1] for i in heads]
    vb, kbg, qd, kend, a_l, attn = [], [], [], [], [], []
    for i in heads:
        qn_b = q_ref[:, sls[i]]
        kn_b = k_ref[:, sls[i]]
        qn = qn_b.astype(F32)
        kn = kn_b.astype(F32)
        beta = beta_all[:, i:i + 1]
        eg = jnp.exp(gcols[i])
        kb = kn * beta
        vb.append(v_ref[:, sls[i]].astype(F32) * beta)
        kbg.append(kb * eg)
        qd.append(qn * eg)
        kend.append(kn * jnp.exp(glasts[i] - gcols[i]))
        grow = gcum_t[HEADS + i:HEADS + i + 1, :]
        dec = jnp.exp(jnp.where(lower_incl, gcols[i] - grow, -jnp.inf))
        a_l.append(jnp.where(lower_strict, _mm_nt(kb, kn_b) * dec, 0.0).astype(BF16))
        attn.append((_mm_nt(qn_b, kn_b) * dec).astype(BF16))
    ns = [x - eye for x in _unit_lower_inverses(a_l, eye, masks)]
    uws = [jnp.dot(ns[i], jnp.concatenate([vb[i], kbg[i]], axis=1).astype(BF16),
                   preferred_element_type=F32) for i in heads]
    us = [vb[i] + uws[i][:, :HEAD_DIM] for i in heads]
    wq = [jnp.concatenate([kbg[i] + uws[i][:, HEAD_DIM:], qd[i]], axis=0).astype(BF16)
          for i in heads]
    ss = [s_ref[i] for i in heads]
    s_bs = [s.astype(BF16) for s in ss]
    zero_s = jnp.zeros((HEAD_DIM, HEAD_DIM), BF16)
    wss = []
    for i in range(0, HEADS, 2):
        s_pair = jnp.concatenate([jnp.concatenate([s_bs[i], zero_s], axis=1),
                                  jnp.concatenate([zero_s, s_bs[i + 1]], axis=1)], axis=0)
        both = jnp.dot(jnp.concatenate([wq[i], wq[i + 1]], axis=1), s_pair,
                       preferred_element_type=F32)
        wss += [both[:, :HEAD_DIM], both[:, HEAD_DIM:]]
    v_new_b = [(us[i] - wss[i][:c, :]).astype(BF16) for i in heads]
    os_ = [wss[i][c:, :] + jnp.dot(attn[i], v_new_b[i], preferred_element_type=F32)
           for i in heads]
    for i in heads:
        s_ref[i] = ss[i] * jnp.exp(glasts[i]) + _mm(kend[i].T, v_new_b[i])
    for i in heads:
        o = os_[i]
        on = o * lax.rsqrt(jnp.mean(o * o, axis=-1, keepdims=True) + EPS) * onw_ref[...]
        y_ref[:, sls[i]] = (on * _silu(z_ref[:, sls[i]])).astype(BF16)


def _gdn(qkv, z, ba, prm, onw, batch, seq):
    n = qkv.shape[0]
    c = GDN_CHUNK
    nt = seq // c

    def row_spec(width, col):
        return pl.BlockSpec((c, width), lambda b, t: (b * nt + t, col))

    return pl.pallas_call(
        _gdn_kernel,
        grid=(batch, nt),
        in_specs=[
            row_spec(WIDTH, 0), row_spec(WIDTH, 1), row_spec(WIDTH, 2),
            row_spec(WIDTH, 0),
            row_spec(LANES, 0),
            pl.BlockSpec((SUBLANES, LANES), lambda b, t: (0, 0)),
            pl.BlockSpec((1, HEAD_DIM), lambda b, t: (0, 0)),
        ],
        out_specs=row_spec(WIDTH, 0),
        out_shape=jax.ShapeDtypeStruct((n, WIDTH), BF16),
        scratch_shapes=[pltpu.VMEM((HEADS, HEAD_DIM, HEAD_DIM), F32)],
        compiler_params=pltpu.CompilerParams(
            dimension_semantics=("parallel", "arbitrary"), vmem_limit_bytes=VMEM_LIMIT_BYTES),
        name="gdn",
    )(qkv, qkv, qkv, z, ba, prm, onw)


def _bf16_terms(x):
    hi = x.astype(BF16).astype(F32)
    rest = x - hi
    mid = rest.astype(BF16).astype(F32)
    return hi, mid, (rest - mid).astype(BF16).astype(F32)


def _proj_fox_kernel(x_ref, nw_ref, w_ref, fb_ref, qw_ref, kw_ref,
                     qo_ref, ko_ref, vt_ref, z_ref, carry_ref):
    tm = x_ref.shape[0]
    cw = MXU_COLS
    per = cw // HEAD_DIM

    @pl.when(pl.program_id(1) == 0)
    def _():
        carry_ref[...] = jnp.zeros_like(carry_ref)

    hn = _normed_rows(x_ref, nw_ref)

    def chunk(col0):
        return jnp.dot(hn, w_ref[:, col0:col0 + cw], preferred_element_type=F32)

    f = jnp.dot(hn, w_ref[:, 4 * WIDTH:4 * WIDTH + LANES], preferred_element_type=F32) + fb_ref[...]
    logf = jnp.minimum(f, 0.0) - jnp.log1p(jnp.exp(-jnp.abs(f)))
    cum = _lane_cumsum(logf.T[0:HEADS, :]) + carry_ref[...]
    carry_ref[...] = cum[:, tm - 1:tm]
    nc = jnp.concatenate([cum * (-LOG2E), jnp.zeros((LANES - HEADS, tm), F32)], axis=0).T

    lane = lax.broadcasted_iota(jnp.int32, (tm, LANES), 1)
    q_tail = jnp.where(lane < 3, 1.0, 0.0).astype(BF16)
    q_gain = qw_ref[...] * (HEAD_DIM ** -0.5 * LOG2E)
    k_gain = kw_ref[...]
    for c in range(WIDTH // cw):
        res = chunk(c * cw)
        for h in range(per):
            a = res[:, h * HEAD_DIM:(h + 1) * HEAD_DIM]
            r = lax.rsqrt(jnp.mean(a * a, axis=-1, keepdims=True) + EPS)
            qo_ref[c * per + h, :, 0:HEAD_DIM] = (a * r * q_gain).astype(BF16)
            qo_ref[c * per + h, :, HEAD_DIM:2 * HEAD_DIM] = q_tail
    for c in range(WIDTH // cw):
        res = chunk(WIDTH + c * cw)
        for h in range(per):
            head = c * per + h
            a = res[:, h * HEAD_DIM:(h + 1) * HEAD_DIM]
            r = lax.rsqrt(jnp.mean(a * a, axis=-1, keepdims=True) + EPS)
            ko_ref[head, :, 0:HEAD_DIM] = (a * r * k_gain).astype(BF16)
            hi, mid, lo = _bf16_terms(jnp.broadcast_to(nc[:, head:head + 1], (tm, LANES)))
            k_tail = jnp.where(lane == 0, hi, jnp.where(lane == 1, mid, jnp.where(lane == 2, lo, 0.0)))
            ko_ref[head, :, HEAD_DIM:2 * HEAD_DIM] = k_tail.astype(BF16)
    for c in range(WIDTH // cw):
        res = chunk(2 * WIDTH + c * cw)
        for h in range(per):
            vt_ref[c * per + h] = res[:, h * HEAD_DIM:(h + 1) * HEAD_DIM].T.astype(BF16)
    for c in range(WIDTH // cw):
        z_ref[:, c * cw:(c + 1) * cw] = chunk(3 * WIDTH + c * cw)


def _proj_fox(x, nw, w, fb, qw, kw, batch, seq):
    n, d = x.shape
    ncol = w.shape[1]
    tm = PROJ_ROWS
    nt = seq // tm
    head_spec = pl.BlockSpec((None, HEADS, tm, 2 * HEAD_DIM), lambda b, t: (b, 0, t, 0))
    return pl.pallas_call(
        _proj_fox_kernel,
        grid=(batch, nt),
        in_specs=[
            pl.BlockSpec((tm, d), lambda b, t: (b * nt + t, 0)),
            pl.BlockSpec((1, d), lambda b, t: (0, 0)),
            pl.BlockSpec((d, ncol), lambda b, t: (0, 0)),
            pl.BlockSpec((1, LANES), lambda b, t: (0, 0)),
            pl.BlockSpec((1, HEAD_DIM), lambda b, t: (0, 0)),
            pl.BlockSpec((1, HEAD_DIM), lambda b, t: (0, 0)),
        ],
        out_specs=[
            head_spec, head_spec,
            pl.BlockSpec((None, HEADS, HEAD_DIM, tm), lambda b, t: (b, 0, 0, t)),
            pl.BlockSpec((tm, WIDTH), lambda b, t: (b * nt + t, 0)),
        ],
        out_shape=[
            jax.ShapeDtypeStruct((batch, HEADS, seq, 2 * HEAD_DIM), BF16),
            jax.ShapeDtypeStruct((batch, HEADS, seq, 2 * HEAD_DIM), BF16),
            jax.ShapeDtypeStruct((batch, HEADS, HEAD_DIM, seq), BF16),
            jax.ShapeDtypeStruct((n, WIDTH), F32),
        ],
        scratch_shapes=[pltpu.VMEM((HEADS, 1), F32)],
        compiler_params=pltpu.CompilerParams(
            dimension_semantics=("parallel", "arbitrary"), vmem_limit_bytes=VMEM_LIMIT_BYTES),
        name="proj_fox",
    )(x, nw, w, fb, qw, kw)


def _fox_attn_kernel(q_ref, k_ref, vt_ref, z_ref, o_ref, m_ref, l_ref, acc_ref, s_ref, *, blk, nsub):
    qi = pl.program_id(1)
    m_ref[...] = jnp.full_like(m_ref, -jnp.inf)
    l_ref[...] = jnp.zeros_like(l_ref)
    acc_ref[...] = jnp.zeros_like(acc_ref)
    pairs = [(d, c) for d in range(nsub) for c in range(nsub)]

    def score(g, half, w):
        d, c = pairs[w]
        start = pl.multiple_of((nsub * g + d) * blk, blk)
        q = q_ref[c * blk:(c + 1) * blk, :]
        k = k_ref[pl.ds(start, blk), :]
        s_ref[half, w] = lax.dot_general(k, q, NT_DIMS, preferred_element_type=F32)

    def update(g, half, w, masked):
        d, c = pairs[w]
        start = pl.multiple_of((nsub * g + d) * blk, blk)
        s = s_ref[half, w]
        if masked:
            key = lax.broadcasted_iota(jnp.int32, (blk, blk), 0)
            qry = lax.broadcasted_iota(jnp.int32, (blk, blk), 1)
            s = jnp.where(qry >= key, s, -jnp.inf)
        m_prev = m_ref[c]
        m_new = jnp.maximum(m_prev, jnp.max(s, axis=0, keepdims=True))
        alpha = jnp.exp2(m_prev - m_new)
        p = jnp.exp2(s - jnp.tile(m_new, (blk // SUBLANES, 1)))
        l_ref[c] = l_ref[c] * alpha + jnp.sum(p, axis=0, keepdims=True)
        acc_ref[c] = acc_ref[c] * jnp.tile(alpha, (HEAD_DIM // SUBLANES, 1)) + jnp.dot(
            vt_ref[:, pl.ds(start, blk)], p.astype(BF16), preferred_element_type=F32)
        m_ref[c] = m_new

    slots = range(len(pairs))

    def group(g, half):
        for w in slots:
            score(g + 1, 1 - half, w)
        for w in slots:
            update(g, half, w, False)

    def diagonal(half):
        for w, (d, c) in enumerate(pairs):
            if c >= d:
                update(qi, half, w, c == d)

    for w in slots:
        score(0, 0, w)

    def body(i, carry):
        group(2 * i, 0)
        group(2 * i + 1, 1)
        return carry

    lax.fori_loop(0, qi // 2, body, 0)

    @pl.when(qi % 2 == 0)
    def _():
        diagonal(0)

    @pl.when(qi % 2 == 1)
    def _():
        group(qi - 1, 0)
        diagonal(1)

    for c in range(nsub):
        o = (acc_ref[c] / jnp.tile(l_ref[c], (HEAD_DIM // SUBLANES, 1))).T
        z = z_ref[c * blk:(c + 1) * blk, :]
        o_ref[c * blk:(c + 1) * blk, :] = (o * _silu(z)).astype(BF16)


def _fox_attn(q, k, vt, z, batch, seq):
    blk = ATTN_BLOCK
    nsub = ATTN_SUBBLOCKS
    tq = blk * nsub
    nq = seq // tq
    bh = batch * HEADS
    out_spec = pl.BlockSpec((tq, HEAD_DIM), lambda g, i: ((g // HEADS) * nq + i, g % HEADS))
    return pl.pallas_call(
        functools.partial(_fox_attn_kernel, blk=blk, nsub=nsub),
        grid=(bh, nq),
        in_specs=[
            pl.BlockSpec((None, tq, 2 * HEAD_DIM), lambda g, i: (g, i, 0)),
            pl.BlockSpec((None, seq, 2 * HEAD_DIM), lambda g, i: (g, 0, 0)),
            pl.BlockSpec((None, HEAD_DIM, seq), lambda g, i: (g, 0, 0)),
            out_spec,
        ],
        out_specs=out_spec,
        out_shape=jax.ShapeDtypeStruct((batch * seq, WIDTH), BF16),
        scratch_shapes=[
            pltpu.VMEM((nsub, SUBLANES, blk), F32),
            pltpu.VMEM((nsub, SUBLANES, blk), F32),
            pltpu.VMEM((nsub, HEAD_DIM, blk), F32),
            pltpu.VMEM((2, nsub * nsub, blk, blk), F32),
        ],
        compiler_params=pltpu.CompilerParams(
            dimension_semantics=("parallel", "arbitrary"), vmem_limit_bytes=VMEM_LIMIT_BYTES),
        name="fox_attn",
    )(q, k, vt, z)


def _pad_cols(w, ncol):
    return jnp.pad(w, ((0, 0), (0, ncol - w.shape[1])))


def _gdn_gate_params(a_log, dt_bias):
    prm = jnp.zeros((SUBLANES, LANES), F32)
    prm = prm.at[0, HEADS:2 * HEADS].set(a_log.astype(F32))
    prm = prm.at[1, HEADS:2 * HEADS].set(dt_bias.astype(F32))
    return prm


def kernel(x, a_norm_w, a_w_in, a_conv_w, a_A_log, a_dt_bias, a_o_norm_w, a_w_out, b_norm_w, b_w_in, b_f_bias, b_q_norm_w, b_k_norm_w, b_w_out, final_norm_w):
    batch, seq, d_model = x.shape
    depth = a_norm_w.shape[0] + b_norm_w.shape[0]
    ncol = 4 * WIDTH + LANES
    h = x.reshape(batch * seq, d_model)
    final_nw = final_norm_w.reshape(1, d_model)
    for i in range(depth):
        j = i // N_MIXERS
        last_nw = final_nw if i == depth - 1 else None
        if i % N_MIXERS == 0:
            taps = jnp.pad(a_conv_w[j].astype(F32), ((0, SUBLANES - CONV_WIDTH), (0, 0)))
            qkv, z, ba = _proj_gdn(h, a_norm_w[j].reshape(1, d_model),
                                   _pad_cols(a_w_in[j], ncol).astype(BF16), taps, batch, seq)
            y = _gdn(qkv, z, ba, _gdn_gate_params(a_A_log[j], a_dt_bias[j]),
                     a_o_norm_w[j].reshape(1, HEAD_DIM), batch, seq)
            h = _proj_residual(y, h, a_w_out[j].astype(BF16), last_nw)
        else:
            fb = jnp.pad(b_f_bias[j].astype(F32), (0, LANES - HEADS)).reshape(1, LANES)
            q, k, vt, z = _proj_fox(h, b_norm_w[j].reshape(1, d_model),
                                    _pad_cols(b_w_in[j], ncol).astype(BF16), fb,
                                    b_q_norm_w[j].reshape(1, HEAD_DIM),
                                    b_k_norm_w[j].reshape(1, HEAD_DIM), batch, seq)
            bh = batch * HEADS
            y = _fox_attn(q.reshape(bh, seq, 2 * HEAD_DIM), k.reshape(bh, seq, 2 * HEAD_DIM),
                          vt.reshape(bh, HEAD_DIM, seq), z, batch, seq)
            h = _proj_residual(y, h, b_w_out[j].astype(BF16), last_nw)
    return h.reshape(batch, seq, d_model)
```

```python
import functools

import jax
import jax.numpy as jnp
from jax import lax
from jax.experimental import pallas as pl
from jax.experimental.pallas import tpu as pltpu

F32 = jnp.float32
BF16 = jnp.bfloat16
EPS = 1e-6

HEADS = 8
HEAD_DIM = 128
WIDTH = HEADS * HEAD_DIM
CONV_WIDTH = 4
N_MIXERS = 2

LANES = 128
SUBLANES = 8
MXU_COLS = 256
VMEM_LIMIT_BYTES = 56 * 1024 * 1024

GDN_CHUNK = 256
GDN_BASE_BLOCK = 16
PROJ_ROWS = 512
ATTN_BLOCK = 512
ATTN_SUBBLOCKS = 2
LOG2E = 1.4426950408889634

NT_DIMS = (((1,), (1,)), ((), ()))


def _sigmoid(x):
    return 0.5 + 0.5 * jnp.tanh(0.5 * x)


def _silu(x):
    h = 0.5 * x
    return h + h * jnp.tanh(h)


def _softplus(x):
    return jnp.maximum(x, 0.0) + jnp.log1p(jnp.exp(-jnp.abs(x)))


def _mm(a, b):
    return jnp.dot(a.astype(BF16), b.astype(BF16), preferred_element_type=F32)


def _mm_nt(a, b):
    return lax.dot_general(a.astype(BF16), b.astype(BF16), NT_DIMS, preferred_element_type=F32)


def _dot_b(a, b):
    return jnp.dot(a, b, preferred_element_type=F32).astype(BF16)


def _normed_rows(x_ref, nw_ref):
    x = x_ref[...]
    ms = jnp.mean(x * x, axis=-1, keepdims=True)
    return (x * lax.rsqrt(ms + EPS) * nw_ref[...]).astype(BF16)


def _lane_cumsum(x):
    n = x.shape[-1]
    lane = lax.broadcasted_iota(jnp.int32, x.shape, x.ndim - 1)
    shift = 1
    while shift < n:
        x = x + jnp.where(lane >= shift, pltpu.roll(x, shift, x.ndim - 1), 0.0)
        shift *= 2
    return x


def _conv_silu_slab(xs_ref, ys_ref, k, taps, tm):
    nph = (tm + SUBLANES) // SUBLANES
    last = CONV_WIDTH - 1
    taps = [jnp.broadcast_to(taps[s:s + 1, :], (SUBLANES, LANES)) for s in range(CONV_WIDTH)]
    ph = [xs_ref[k, pl.ds(r, SUBLANES, stride=nph), :] for r in range(nph)]
    wrapped = {r: pltpu.roll(ph[r], 1, 0) for r in range(nph - last, nph)}
    for r in range(nph):
        acc = ph[r] * taps[last]
        for s in range(1, CONV_WIDTH):
            src = ph[r - s] if r >= s else wrapped[r - s + nph]
            acc = acc + src * taps[last - s]
        ys_ref[k, pl.ds(r, SUBLANES, stride=nph), :] = _silu(acc)


def _proj_gdn_kernel(x_ref, nw_ref, w_ref, cw_ref, qkv_ref, z_ref, ba_ref, tail_ref, xs_ref, ys_ref):
    tm = x_ref.shape[0]
    cw = MXU_COLS
    per = cw // LANES
    assert (tm + SUBLANES) // SUBLANES % 2 == 1

    @pl.when(pl.program_id(1) == 0)
    def _():
        tail_ref[...] = jnp.zeros_like(tail_ref)

    hn = _normed_rows(x_ref, nw_ref)
    n_conv = 3 * WIDTH // cw

    def chunk_dot(c):
        return jnp.dot(hn, w_ref[:, c * cw:(c + 1) * cw], preferred_element_type=F32)

    pre_next = chunk_dot(0)
    for c in range(n_conv):
        pre = pre_next
        if c + 1 < n_conv:
            pre_next = chunk_dot(c + 1)
        for h in range(per):
            slab = c * per + h
            k = (c % 2) * per + h
            lanes = slice(slab * LANES, (slab + 1) * LANES)
            pre_h = pre[:, h * LANES:(h + 1) * LANES]
            xs_ref[k, 0:SUBLANES, :] = tail_ref[slab]
            xs_ref[k, SUBLANES:SUBLANES + tm, :] = pre_h
            tail_ref[slab] = pre_h[tm - SUBLANES:tm, :]
            _conv_silu_slab(xs_ref, ys_ref, k, cw_ref[:, lanes], tm)
            act = ys_ref[k, SUBLANES:SUBLANES + tm, :]
            if slab < 2 * HEADS:
                scale = HEAD_DIM ** -0.5 if slab < HEADS else 1.0
                r = lax.rsqrt(jnp.sum(act * act, axis=-1, keepdims=True) + EPS)
                act = act * (r * scale)
            qkv_ref[:, lanes] = act.astype(BF16)
    for c in range(WIDTH // cw):
        cols = slice(3 * WIDTH + c * cw, 3 * WIDTH + (c + 1) * cw)
        z_ref[:, c * cw:(c + 1) * cw] = jnp.dot(hn, w_ref[:, cols], preferred_element_type=F32)
    ba_ref[...] = jnp.dot(hn, w_ref[:, 4 * WIDTH:4 * WIDTH + LANES], preferred_element_type=F32)


def _proj_gdn(x, nw, w, cw, batch, seq):
    n, d = x.shape
    ncol = w.shape[1]
    tm = PROJ_ROWS
    nt = seq // tm

    def row_spec(width):
        return pl.BlockSpec((tm, width), lambda b, t: (b * nt + t, 0))

    return pl.pallas_call(
        _proj_gdn_kernel,
        grid=(batch, nt),
        in_specs=[
            row_spec(d),
            pl.BlockSpec((1, d), lambda b, t: (0, 0)),
            pl.BlockSpec((d, ncol), lambda b, t: (0, 0)),
            pl.BlockSpec((SUBLANES, 3 * WIDTH), lambda b, t: (0, 0)),
        ],
        out_specs=[row_spec(3 * WIDTH), row_spec(WIDTH), row_spec(LANES)],
        out_shape=[
            jax.ShapeDtypeStruct((n, 3 * WIDTH), BF16),
            jax.ShapeDtypeStruct((n, WIDTH), F32),
            jax.ShapeDtypeStruct((n, LANES), F32),
        ],
        scratch_shapes=[
            pltpu.VMEM((3 * WIDTH // LANES, SUBLANES, LANES), F32),
            pltpu.VMEM((2 * MXU_COLS // LANES, tm + SUBLANES, LANES), F32),
            pltpu.VMEM((2 * MXU_COLS // LANES, tm + SUBLANES, LANES), F32),
        ],
        compiler_params=pltpu.CompilerParams(
            dimension_semantics=("parallel", "arbitrary"), vmem_limit_bytes=VMEM_LIMIT_BYTES),
        name="proj_gdn",
    )(x, nw, w, cw)


def _proj_residual_kernel(y_ref, r_ref, w_ref, o_ref):
    o_ref[...] = r_ref[...] + jnp.dot(y_ref[...], w_ref[...], preferred_element_type=F32)


def _proj_residual_norm_kernel(y_ref, r_ref, w_ref, nw_ref, o_ref):
    h = r_ref[...] + jnp.dot(y_ref[...], w_ref[...], preferred_element_type=F32)
    ms = jnp.mean(h * h, axis=-1, keepdims=True)
    o_ref[...] = h * lax.rsqrt(ms + EPS) * nw_ref[...]


def _proj_residual(y, res, w, final_nw=None):
    n, d_in = y.shape
    d_out = w.shape[1]
    tm = PROJ_ROWS
    in_specs = [
        pl.BlockSpec((tm, d_in), lambda i: (i, 0)),
        pl.BlockSpec((tm, d_out), lambda i: (i, 0)),
        pl.BlockSpec((d_in, d_out), lambda i: (0, 0)),
    ]
    args = [y, res, w]
    body = _proj_residual_kernel
    if final_nw is not None:
        in_specs.append(pl.BlockSpec((1, d_out), lambda i: (0, 0)))
        args.append(final_nw)
        body = _proj_residual_norm_kernel
    return pl.pallas_call(
        body,
        grid=(n // tm,),
        in_specs=in_specs,
        out_specs=pl.BlockSpec((tm, d_out), lambda i: (i, 0)),
        out_shape=jax.ShapeDtypeStruct((n, d_out), F32),
        compiler_params=pltpu.CompilerParams(
            dimension_semantics=("parallel",), vmem_limit_bytes=VMEM_LIMIT_BYTES),
        name="proj_residual",
    )(*args)


def _block_masks(c):
    row = lax.broadcasted_iota(jnp.int32, (c, c), 0)
    col = lax.broadcasted_iota(jnp.int32, (c, c), 1)
    xr = row ^ col
    eye = jnp.where(row == col, 1.0, 0.0).astype(BF16)
    b = GDN_BASE_BLOCK
    below = jnp.where(xr < b, 1.0, 0.0)
    masks = [below.astype(BF16)]
    while b < c:
        below_next = jnp.where(xr < 2 * b, 1.0, 0.0)
        masks.append((below_next - below).astype(BF16))
        below = below_next
        b *= 2
    return eye, masks


def _unit_lower_inverses(a_list, eye, masks):
    c = eye.shape[0]
    ps = [a * masks[0] for a in a_list]
    xs = [eye - p for p in ps]
    width = 2
    while width < GDN_BASE_BLOCK:
        ps = [_dot_b(p, p) for p in ps]
        xs = [x + _dot_b(x, p) for x, p in zip(xs, ps)]
        width *= 2
    b = GDN_BASE_BLOCK
    for m in masks[1:]:
        starts = range(0, c, 2 * b)
        lows = [jnp.concatenate([x[s + b:s + 2 * b, :] for s in starts], axis=0) for x in xs]
        ys = [_dot_b(lo, a * m) for lo, a in zip(lows, a_list)]
        lows = [lo - _dot_b(y, x) for lo, y, x in zip(lows, ys, xs)]
        xs = [jnp.concatenate([piece for k, s in enumerate(starts)
                               for piece in (x[s:s + b, :], lo[k * b:(k + 1) * b, :])], axis=0)
              for x, lo in zip(xs, lows)]
        b *= 2
    return xs


def _gdn_kernel(q_ref, k_ref, v_ref, z_ref, ba_ref, prm_ref, onw_ref, y_ref, s_ref):
    c = q_ref.shape[0]

    @pl.when(pl.program_id(1) == 0)
    def _():
        s_ref[...] = jnp.zeros_like(s_ref)

    row = lax.broadcasted_iota(jnp.int32, (c, c), 0)
    col = lax.broadcasted_iota(jnp.int32, (c, c), 1)
    lower_incl = row >= col
    lower_strict = row > col
    eye, masks = _block_masks(c)

    ba = ba_ref[...]
    neg_a = -jnp.exp(prm_ref[0:1, :])
    beta_all = _sigmoid(ba)
    glog = neg_a * _softplus(ba + prm_ref[1:2, :])
    gcum_t = _lane_cumsum(glog.T[0:2 * HEADS, :])
    gcum = jnp.concatenate([gcum_t, jnp.zeros((LANES - 2 * HEADS, c), F32)], axis=0).T

    heads = range(HEADS)
    sls = [slice(i * HEAD_DIM, (i + 1) * HEAD_DIM) for i in heads]
    gcols = [gcum[:, HEADS + i:HEADS + i + 1] for i in heads]
    glasts = [gcum[c - 1:c, HEADS + i:HEADS + i + 1] for i in heads]
    vb, kbg, qd, kend, a_l, attn = [], [], [], [], [], []
    for i in heads:
        qn_b = q_ref[:, sls[i]]
        kn_b = k_ref[:, sls[i]]
        qn = qn_b.astype(F32)
        kn = kn_b.astype(F32)
        beta = beta_all[:, i:i + 1]
        eg = jnp.exp(gcols[i])
        kb = kn * beta
        vb.append(v_ref[:, sls[i]].astype(F32) * beta)
        kbg.append(kb * eg)
        qd.append(qn * eg)
        kend.append(kn * jnp.exp(glasts[i] - gcols[i]))
        grow = gcum_t[HEADS + i:HEADS + i + 1, :]
        dec = jnp.exp(jnp.where(lower_incl, gcols[i] - grow, -jnp.inf))
        gram = _mm_nt(jnp.concatenate([kb.astype(BF16), qn_b], axis=0), kn_b)
        a_l.append(jnp.where(lower_strict, gram[:c, :] * dec, 0.0).astype(BF16))
        attn.append((gram[c:, :] * dec).astype(BF16))
    ns = [x - eye for x in _unit_lower_inverses(a_l, eye, masks)]
    uws = [jnp.dot(ns[i], jnp.concatenate([vb[i], kbg[i]], axis=1).astype(BF16),
                   preferred_element_type=F32) for i in heads]
    us = [vb[i] + uws[i][:, :HEAD_DIM] for i in heads]
    wq = [jnp.concatenate([kbg[i] + uws[i][:, HEAD_DIM:], qd[i]], axis=0).astype(BF16)
          for i in heads]
    ss = [s_ref[i] for i in heads]
    s_bs = [s.astype(BF16) for s in ss]
    zero_s = jnp.zeros((HEAD_DIM, HEAD_DIM), BF16)
    wss = []
    for i in range(0, HEADS, 2):
        s_pair = jnp.concatenate([jnp.concatenate([s_bs[i], zero_s], axis=1),
                                  jnp.concatenate([zero_s, s_bs[i + 1]], axis=1)], axis=0)
        both = jnp.dot(jnp.concatenate([wq[i], wq[i + 1]], axis=1), s_pair,
                       preferred_element_type=F32)
        wss += [both[:, :HEAD_DIM], both[:, HEAD_DIM:]]
    v_new_b = [(us[i] - wss[i][:c, :]).astype(BF16) for i in heads]
    os_ = [wss[i][c:, :] + jnp.dot(attn[i], v_new_b[i], preferred_element_type=F32)
           for i in heads]
    for i in heads:
        s_ref[i] = ss[i] * jnp.exp(glasts[i]) + _mm(kend[i].T, v_new_b[i])
    for i in heads:
        o = os_[i]
        on = o * lax.rsqrt(jnp.mean(o * o, axis=-1, keepdims=True) + EPS) * onw_ref[...]
        y_ref[:, sls[i]] = (on * _silu(z_ref[:, sls[i]])).astype(BF16)


def _gdn(qkv, z, ba, prm, onw, batch, seq):
    n = qkv.shape[0]
    c = GDN_CHUNK
    nt = seq // c

    def row_spec(width, col):
        return pl.BlockSpec((c, width), lambda b, t: (b * nt + t, col))

    return pl.pallas_call(
        _gdn_kernel,
        grid=(batch, nt),
        in_specs=[
            row_spec(WIDTH, 0), row_spec(WIDTH, 1), row_spec(WIDTH, 2),
            row_spec(WIDTH, 0),
            row_spec(LANES, 0),
            pl.BlockSpec((SUBLANES, LANES), lambda b, t: (0, 0)),
            pl.BlockSpec((1, HEAD_DIM), lambda b, t: (0, 0)),
        ],
        out_specs=row_spec(WIDTH, 0),
        out_shape=jax.ShapeDtypeStruct((n, WIDTH), BF16),
        scratch_shapes=[pltpu.VMEM((HEADS, HEAD_DIM, HEAD_DIM), F32)],
        compiler_params=pltpu.CompilerParams(
            dimension_semantics=("parallel", "arbitrary"), vmem_limit_bytes=VMEM_LIMIT_BYTES),
        name="gdn",
    )(qkv, qkv, qkv, z, ba, prm, onw)


def _bf16_terms(x):
    hi = x.astype(BF16).astype(F32)
    rest = x - hi
    mid = rest.astype(BF16).astype(F32)
    return hi, mid, (rest - mid).astype(BF16).astype(F32)


def _proj_fox_kernel(y_ref, r_ref, wo_ref, nw_ref, w_ref, fb_ref, qw_ref, kw_ref,
                     h_ref, qt_ref, ko_ref, vt_ref, z_ref, carry_ref):
    tm = r_ref.shape[0]
    cw = MXU_COLS
    per = cw // HEAD_DIM

    @pl.when(pl.program_id(1) == 0)
    def _():
        carry_ref[...] = jnp.zeros_like(carry_ref)

    x = r_ref[...] + jnp.dot(y_ref[...], wo_ref[...], preferred_element_type=F32)
    h_ref[...] = x
    ms = jnp.mean(x * x, axis=-1, keepdims=True)
    hn = (x * lax.rsqrt(ms + EPS) * nw_ref[...]).astype(BF16)

    def chunk(col0):
        return jnp.dot(hn, w_ref[:, col0:col0 + cw], preferred_element_type=F32)

    f = jnp.dot(hn, w_ref[:, 4 * WIDTH:4 * WIDTH + LANES], preferred_element_type=F32) + fb_ref[...]
    logf = jnp.minimum(f, 0.0) - jnp.log1p(jnp.exp(-jnp.abs(f)))
    cum = _lane_cumsum(logf.T[0:HEADS, :]) + carry_ref[...]
    carry_ref[...] = cum[:, tm - 1:tm]
    nc = jnp.concatenate([cum * (-LOG2E), jnp.zeros((LANES - HEADS, tm), F32)], axis=0).T

    lane = lax.broadcasted_iota(jnp.int32, (tm, LANES), 1)
    sub = lax.broadcasted_iota(jnp.int32, (HEAD_DIM, tm), 0)
    q_tail = jnp.where(sub < 3, 1.0, 0.0).astype(BF16)
    q_gain = qw_ref[...] * (HEAD_DIM ** -0.5 * LOG2E)
    k_gain = kw_ref[...]
    for c in range(WIDTH // cw):
        res = chunk(c * cw)
        for h in range(per):
            a = res[:, h * HEAD_DIM:(h + 1) * HEAD_DIM]
            r = lax.rsqrt(jnp.mean(a * a, axis=-1, keepdims=True) + EPS)
            qt_ref[c * per + h, 0:HEAD_DIM, :] = (a * r * q_gain).T.astype(BF16)
            qt_ref[c * per + h, HEAD_DIM:2 * HEAD_DIM, :] = q_tail
    for c in range(WIDTH // cw):
        res = chunk(WIDTH + c * cw)
        for h in range(per):
            head = c * per + h
            a = res[:, h * HEAD_DIM:(h + 1) * HEAD_DIM]
            r = lax.rsqrt(jnp.mean(a * a, axis=-1, keepdims=True) + EPS)
            ko_ref[head, :, 0:HEAD_DIM] = (a * r * k_gain).astype(BF16)
            hi, mid, lo = _bf16_terms(jnp.broadcast_to(nc[:, head:head + 1], (tm, LANES)))
            k_tail = jnp.where(lane == 0, hi, jnp.where(lane == 1, mid, jnp.where(lane == 2, lo, 0.0)))
            ko_ref[head, :, HEAD_DIM:2 * HEAD_DIM] = k_tail.astype(BF16)
    for c in range(WIDTH // cw):
        res = chunk(2 * WIDTH + c * cw)
        for h in range(per):
            vt_ref[c * per + h] = res[:, h * HEAD_DIM:(h + 1) * HEAD_DIM].T.astype(BF16)
    for c in range(WIDTH // cw):
        z_ref[:, c * cw:(c + 1) * cw] = chunk(3 * WIDTH + c * cw)


def _proj_fox(y, res, wo, nw, w, fb, qw, kw, batch, seq):
    n, d = res.shape
    d_y = y.shape[1]
    ncol = w.shape[1]
    tm = PROJ_ROWS
    nt = seq // tm

    def row_spec(width):
        return pl.BlockSpec((tm, width), lambda b, t: (b * nt + t, 0))

    def const_spec(shape):
        return pl.BlockSpec(shape, lambda b, t: (0, 0), pipeline_mode=pl.Buffered(1))

    return pl.pallas_call(
        _proj_fox_kernel,
        grid=(batch, nt),
        in_specs=[
            row_spec(d_y), row_spec(d),
            const_spec((d_y, d)), const_spec((1, d)), const_spec((d, ncol)),
            const_spec((1, LANES)), const_spec((1, HEAD_DIM)), const_spec((1, HEAD_DIM)),
        ],
        out_specs=[
            row_spec(d),
            pl.BlockSpec((None, HEADS, 2 * HEAD_DIM, tm), lambda b, t: (b, 0, 0, t)),
            pl.BlockSpec((None, HEADS, tm, 2 * HEAD_DIM), lambda b, t: (b, 0, t, 0)),
            pl.BlockSpec((None, HEADS, HEAD_DIM, tm), lambda b, t: (b, 0, 0, t)),
            row_spec(WIDTH),
        ],
        out_shape=[
            jax.ShapeDtypeStruct((n, d), F32),
            jax.ShapeDtypeStruct((batch, HEADS, 2 * HEAD_DIM, seq), BF16),
            jax.ShapeDtypeStruct((batch, HEADS, seq, 2 * HEAD_DIM), BF16),
            jax.ShapeDtypeStruct((batch, HEADS, HEAD_DIM, seq), BF16),
            jax.ShapeDtypeStruct((n, WIDTH), F32),
        ],
        scratch_shapes=[pltpu.VMEM((HEADS, 1), F32)],
        compiler_params=pltpu.CompilerParams(
            dimension_semantics=("parallel", "arbitrary"), vmem_limit_bytes=VMEM_LIMIT_BYTES),
        name="proj_fox",
    )(y, res, wo, nw, w, fb, qw, kw)


def _fox_attn_kernel(qt_ref, k_ref, vt_ref, z_ref, o_ref, m_ref, l_ref, acc_ref, s_ref, *, blk, nsub):
    qi = pl.program_id(1)
    m_ref[...] = jnp.full_like(m_ref, -jnp.inf)
    l_ref[...] = jnp.zeros_like(l_ref)
    acc_ref[...] = jnp.zeros_like(acc_ref)
    pairs = [(d, c) for d in range(nsub) for c in range(nsub)]

    def score(g, half, w):
        d, c = pairs[w]
        start = pl.multiple_of((nsub * g + d) * blk, blk)
        s_ref[half, w] = jnp.dot(k_ref[pl.ds(start, blk), :], qt_ref[:, c * blk:(c + 1) * blk],
                                 preferred_element_type=F32)

    def update(g, half, w, masked):
        d, c = pairs[w]
        start = pl.multiple_of((nsub * g + d) * blk, blk)
        s = s_ref[half, w]
        if masked:
            key = lax.broadcasted_iota(jnp.int32, (blk, blk), 0)
            qry = lax.broadcasted_iota(jnp.int32, (blk, blk), 1)
            s = jnp.where(qry >= key, s, -jnp.inf)
        m_prev = m_ref[c]
        m_new = jnp.maximum(m_prev, jnp.max(s, axis=0, keepdims=True))
        alpha = jnp.exp2(m_prev - m_new)
        p = jnp.exp2(s - jnp.tile(m_new, (blk // SUBLANES, 1)))
        l_ref[c] = l_ref[c] * alpha + jnp.sum(p, axis=0, keepdims=True)
        acc_ref[c] = acc_ref[c] * jnp.tile(alpha, (HEAD_DIM // SUBLANES, 1)) + jnp.dot(
            vt_ref[:, pl.ds(start, blk)], p.astype(BF16), preferred_element_type=F32)
        m_ref[c] = m_new

    slots = range(len(pairs))

    def group(g, half):
        for w in slots:
            score(g + 1, 1 - half, w)
        for w in slots:
            update(g, half, w, False)

    def diagonal(half):
        for w, (d, c) in enumerate(pairs):
            if c >= d:
                update(qi, half, w, c == d)

    for w in slots:
        score(0, 0, w)

    def body(i, carry):
        group(2 * i, 0)
        group(2 * i + 1, 1)
        return carry

    lax.fori_loop(0, qi // 2, body, 0)

    @pl.when(qi % 2 == 0)
    def _():
        diagonal(0)

    @pl.when(qi % 2 == 1)
    def _():
        group(qi - 1, 0)
        diagonal(1)

    for c in range(nsub):
        o = (acc_ref[c] / jnp.tile(l_ref[c], (HEAD_DIM // SUBLANES, 1))).T
        z = z_ref[c * blk:(c + 1) * blk, :]
        o_ref[c * blk:(c + 1) * blk, :] = (o * _silu(z)).astype(BF16)


def _fox_attn(q, k, vt, z, batch, seq):
    blk = ATTN_BLOCK
    nsub = ATTN_SUBBLOCKS
    tq = blk * nsub
    nq = seq // tq
    bh = batch * HEADS
    out_spec = pl.BlockSpec((tq, HEAD_DIM), lambda g, i: ((g // HEADS) * nq + i, g % HEADS))
    return pl.pallas_call(
        functools.partial(_fox_attn_kernel, blk=blk, nsub=nsub),
        grid=(bh, nq),
        in_specs=[
            pl.BlockSpec((None, 2 * HEAD_DIM, tq), lambda g, i: (g, 0, i)),
            pl.BlockSpec((None, seq, 2 * HEAD_DIM), lambda g, i: (g, 0, 0)),
            pl.BlockSpec((None, HEAD_DIM, seq), lambda g, i: (g, 0, 0)),
            out_spec,
        ],
        out_specs=out_spec,
        out_shape=jax.ShapeDtypeStruct((batch * seq, WIDTH), BF16),
        scratch_shapes=[
            pltpu.VMEM((nsub, SUBLANES, blk), F32),
            pltpu.VMEM((nsub, SUBLANES, blk), F32),
            pltpu.VMEM((nsub, HEAD_DIM, blk), F32),
            pltpu.VMEM((2, nsub * nsub, blk, blk), F32),
        ],
        compiler_params=pltpu.CompilerParams(
            dimension_semantics=("parallel", "arbitrary"), vmem_limit_bytes=VMEM_LIMIT_BYTES),
        name="fox_attn",
    )(q, k, vt, z)


def _pad_cols(w, ncol):
    return jnp.pad(w, ((0, 0), (0, ncol - w.shape[1])))


def _gdn_gate_params(a_log, dt_bias):
    prm = jnp.zeros((SUBLANES, LANES), F32)
    prm = prm.at[0, HEADS:2 * HEADS].set(a_log.astype(F32))
    prm = prm.at[1, HEADS:2 * HEADS].set(dt_bias.astype(F32))
    return prm


def kernel(x, a_norm_w, a_w_in, a_conv_w, a_A_log, a_dt_bias, a_o_norm_w, a_w_out, b_norm_w, b_w_in, b_f_bias, b_q_norm_w, b_k_norm_w, b_w_out, final_norm_w):
    batch, seq, d_model = x.shape
    depth = a_norm_w.shape[0] + b_norm_w.shape[0]
    ncol = 4 * WIDTH + LANES
    h = x.reshape(batch * seq, d_model)
    final_nw = final_norm_w.reshape(1, d_model)
    pending = None
    for i in range(depth):
        j = i // N_MIXERS
        last_nw = final_nw if i == depth - 1 else None
        if i % N_MIXERS == 0:
            taps = jnp.pad(a_conv_w[j].astype(F32), ((0, SUBLANES - CONV_WIDTH), (0, 0)))
            qkv, z, ba = _proj_gdn(h, a_norm_w[j].reshape(1, d_model),
                                   _pad_cols(a_w_in[j], ncol).astype(BF16), taps, batch, seq)
            y = _gdn(qkv, z, ba, _gdn_gate_params(a_A_log[j], a_dt_bias[j]),
                     a_o_norm_w[j].reshape(1, HEAD_DIM), batch, seq)
            if i + 1 < depth:
                pending = (y, a_w_out[j].astype(BF16))
            else:
                h = _proj_residual(y, h, a_w_out[j].astype(BF16), last_nw)
        else:
            y_prev, wo_prev = pending
            pending = None
            fb = jnp.pad(b_f_bias[j].astype(F32), (0, LANES - HEADS)).reshape(1, LANES)
            h, qt, k, vt, z = _proj_fox(y_prev, h, wo_prev, b_norm_w[j].reshape(1, d_model),
                                        _pad_cols(b_w_in[j], ncol).astype(BF16), fb,
                                        b_q_norm_w[j].reshape(1, HEAD_DIM),
                                        b_k_norm_w[j].reshape(1, HEAD_DIM), batch, seq)
            bh = batch * HEADS
            y = _fox_attn(qt.reshape(bh, 2 * HEAD_DIM, seq), k.reshape(bh, seq, 2 * HEAD_DIM),
                          vt.reshape(bh, HEAD_DIM, seq), z, batch, seq)
            h = _proj_residual(y, h, b_w_out[j].astype(BF16), last_nw)
    return h.reshape(batch, seq, d_model)
```

```python
import functools

import jax
import jax.numpy as jnp
from jax import lax
from jax.experimental import pallas as pl
from jax.experimental.pallas import tpu as pltpu

F32 = jnp.float32
BF16 = jnp.bfloat16
EPS = 1e-6

HEADS = 8
HEAD_DIM = 128
WIDTH = HEADS * HEAD_DIM
CONV_WIDTH = 4
N_MIXERS = 2

LANES = 128
SUBLANES = 8
MXU_COLS = 256
VMEM_LIMIT_BYTES = 56 * 1024 * 1024

GDN_CHUNK = 256
GDN_BASE_BLOCK = 16
PROJ_ROWS = 512
GDN_PROJ_ROWS = 512
RESIDUAL_ROWS = 1024
ATTN_BLOCK = 512
ATTN_SUBBLOCKS = 2
V_ROWS = HEAD_DIM + 16
LOG2E = 1.4426950408889634

NT_DIMS = (((1,), (1,)), ((), ()))


def _sigmoid(x):
    return 0.5 + 0.5 * jnp.tanh(0.5 * x)


def _silu(x):
    h = 0.5 * x
    return h + h * jnp.tanh(h)


def _softplus(x):
    return jnp.maximum(x, 0.0) + jnp.log(1.0 + jnp.exp(-jnp.abs(x)))


def _mm(a, b):
    return jnp.dot(a.astype(BF16), b.astype(BF16), preferred_element_type=F32)


def _mm_nt(a, b):
    return lax.dot_general(a.astype(BF16), b.astype(BF16), NT_DIMS, preferred_element_type=F32)


def _dot_b(a, b):
    return jnp.dot(a, b, preferred_element_type=F32).astype(BF16)


def _normed_rows(x_ref, nw_ref):
    x = x_ref[...]
    ms = jnp.mean(x * x, axis=-1, keepdims=True)
    return (x * lax.rsqrt(ms + EPS) * nw_ref[...]).astype(BF16)


def _lane_cumsum(x):
    n = x.shape[-1]
    lane = lax.broadcasted_iota(jnp.int32, x.shape, x.ndim - 1)
    shift = 1
    while shift < n:
        x = x + jnp.where(lane >= shift, pltpu.roll(x, shift, x.ndim - 1), 0.0)
        shift *= 2
    return x


def _conv_silu_slab(xs_ref, ys_ref, k, taps, tm):
    nph = (tm + SUBLANES) // SUBLANES
    last = CONV_WIDTH - 1
    taps = [jnp.broadcast_to(taps[s:s + 1, :], (SUBLANES, LANES)) for s in range(CONV_WIDTH)]
    ph = [xs_ref[k, pl.ds(r, SUBLANES, stride=nph), :] for r in range(nph)]
    wrapped = {r: pltpu.roll(ph[r], 1, 0) for r in range(nph - last, nph)}
    for r in range(nph):
        acc = ph[r] * taps[last]
        for s in range(1, CONV_WIDTH):
            src = ph[r - s] if r >= s else wrapped[r - s + nph]
            acc = acc + src * taps[last - s]
        ys_ref[k, pl.ds(r, SUBLANES, stride=nph), :] = _silu(acc)


def _proj_gdn_kernel(x_ref, nw_ref, w_ref, cw_ref, qkv_ref, z_ref, ba_ref, tail_ref, xs_ref, ys_ref):
    tm = x_ref.shape[0]
    cw = MXU_COLS
    per = cw // LANES
    assert (tm + SUBLANES) // SUBLANES % 2 == 1

    @pl.when(pl.program_id(1) == 0)
    def _():
        tail_ref[...] = jnp.zeros_like(tail_ref)

    hn = _normed_rows(x_ref, nw_ref)
    n_conv = 3 * WIDTH // cw

    def chunk_dot(c):
        return jnp.dot(hn, w_ref[:, c * cw:(c + 1) * cw], preferred_element_type=F32)

    pre_next = chunk_dot(0)
    for c in range(n_conv):
        pre = pre_next
        if c + 1 < n_conv:
            pre_next = chunk_dot(c + 1)
        for h in range(per):
            slab = c * per + h
            k = (c % 2) * per + h
            lanes = slice(slab * LANES, (slab + 1) * LANES)
            pre_h = pre[:, h * LANES:(h + 1) * LANES]
            xs_ref[k, 0:SUBLANES, :] = tail_ref[slab]
            xs_ref[k, SUBLANES:SUBLANES + tm, :] = pre_h
            tail_ref[slab] = pre_h[tm - SUBLANES:tm, :]
            _conv_silu_slab(xs_ref, ys_ref, k, cw_ref[:, lanes], tm)
            act = ys_ref[k, SUBLANES:SUBLANES + tm, :]
            if slab < 2 * HEADS:
                scale = HEAD_DIM ** -0.5 if slab < HEADS else 1.0
                r = lax.rsqrt(jnp.sum(act * act, axis=-1, keepdims=True) + EPS)
                act = act * (r * scale)
            qkv_ref[:, lanes] = act.astype(BF16)
    for c in range(WIDTH // cw):
        cols = slice(3 * WIDTH + c * cw, 3 * WIDTH + (c + 1) * cw)
        z_ref[:, c * cw:(c + 1) * cw] = jnp.dot(hn, w_ref[:, cols], preferred_element_type=F32)
    ba_ref[...] = jnp.dot(hn, w_ref[:, 4 * WIDTH:4 * WIDTH + LANES], preferred_element_type=F32)


def _proj_gdn(x, nw, w, cw, batch, seq):
    n, d = x.shape
    ncol = w.shape[1]
    tm = GDN_PROJ_ROWS
    nt = seq // tm

    def row_spec(width):
        return pl.BlockSpec((tm, width), lambda b, t: (b * nt + t, 0))

    return pl.pallas_call(
        _proj_gdn_kernel,
        grid=(batch, nt),
        in_specs=[
            row_spec(d),
            pl.BlockSpec((1, d), lambda b, t: (0, 0)),
            pl.BlockSpec((d, ncol), lambda b, t: (0, 0)),
            pl.BlockSpec((SUBLANES, 3 * WIDTH), lambda b, t: (0, 0)),
        ],
        out_specs=[row_spec(3 * WIDTH), row_spec(WIDTH), row_spec(LANES)],
        out_shape=[
            jax.ShapeDtypeStruct((n, 3 * WIDTH), BF16),
            jax.ShapeDtypeStruct((n, WIDTH), F32),
            jax.ShapeDtypeStruct((n, LANES), F32),
        ],
        scratch_shapes=[
            pltpu.VMEM((3 * WIDTH // LANES, SUBLANES, LANES), F32),
            pltpu.VMEM((2 * MXU_COLS // LANES, tm + SUBLANES, LANES), F32),
            pltpu.VMEM((2 * MXU_COLS // LANES, tm + SUBLANES, LANES), F32),
        ],
        compiler_params=pltpu.CompilerParams(
            dimension_semantics=("parallel", "arbitrary"), vmem_limit_bytes=VMEM_LIMIT_BYTES),
        name="proj_gdn",
    )(x, nw, w, cw)


def _proj_residual_kernel(y_ref, r_ref, w_ref, o_ref):
    o_ref[...] = r_ref[...] + jnp.dot(y_ref[...], w_ref[...], preferred_element_type=F32)


def _proj_residual_norm_kernel(y_ref, r_ref, w_ref, nw_ref, o_ref):
    h = r_ref[...] + jnp.dot(y_ref[...], w_ref[...], preferred_element_type=F32)
    ms = jnp.mean(h * h, axis=-1, keepdims=True)
    o_ref[...] = h * lax.rsqrt(ms + EPS) * nw_ref[...]


def _proj_residual(y, res, w, final_nw=None):
    n, d_in = y.shape
    d_out = w.shape[1]
    tm = RESIDUAL_ROWS
    in_specs = [
        pl.BlockSpec((tm, d_in), lambda i: (i, 0)),
        pl.BlockSpec((tm, d_out), lambda i: (i, 0)),
        pl.BlockSpec((d_in, d_out), lambda i: (0, 0)),
    ]
    args = [y, res, w]
    body = _proj_residual_kernel
    if final_nw is not None:
        in_specs.append(pl.BlockSpec((1, d_out), lambda i: (0, 0)))
        args.append(final_nw)
        body = _proj_residual_norm_kernel
    return pl.pallas_call(
        body,
        grid=(n // tm,),
        in_specs=in_specs,
        out_specs=pl.BlockSpec((tm, d_out), lambda i: (i, 0)),
        out_shape=jax.ShapeDtypeStruct((n, d_out), F32),
        compiler_params=pltpu.CompilerParams(
            dimension_semantics=("parallel",), vmem_limit_bytes=VMEM_LIMIT_BYTES),
        name="proj_residual",
    )(*args)


def _block_masks(c):
    row = lax.broadcasted_iota(jnp.int32, (c, c), 0)
    col = lax.broadcasted_iota(jnp.int32, (c, c), 1)
    xr = row ^ col
    eye_f = jnp.where(row == col, 1.0, 0.0)
    eye = eye_f.astype(BF16)
    b = GDN_BASE_BLOCK
    below = jnp.where(xr < b, 1.0, 0.0)
    masks = [(below - eye_f).astype(BF16)]
    while b < c:
        below_next = jnp.where(xr < 2 * b, 1.0, 0.0)
        masks.append((below_next - below).astype(BF16))
        below = below_next
        b *= 2
    return eye, masks


def _unit_lower_inverses(a_list, eye, masks):
    c = eye.shape[0]
    ps = [a * masks[0] for a in a_list]
    xs = [eye - p for p in ps]
    width = 2
    while width < GDN_BASE_BLOCK:
        ps = [_dot_b(p, p) for p in ps]
        xs = [x + _dot_b(x, p) for x, p in zip(xs, ps)]
        width *= 2
    b = GDN_BASE_BLOCK
    for m in masks[1:]:
        starts = range(0, c, 2 * b)
        lows = [jnp.concatenate([x[s + b:s + 2 * b, :] for s in starts], axis=0) for x in xs]
        ys = [_dot_b(lo, a * m) for lo, a in zip(lows, a_list)]
        lows = [lo - _dot_b(y, x) for lo, y, x in zip(lows, ys, xs)]
        xs = [jnp.concatenate([piece for k, s in enumerate(starts)
                               for piece in (x[s:s + b, :], lo[k * b:(k + 1) * b, :])], axis=0)
              for x, lo in zip(xs, lows)]
        b *= 2
    return xs


def _gdn_kernel(q_ref, k_ref, v_ref, z_ref, ba_ref, prm_ref, onw_ref, y_ref, s_ref):
    c = q_ref.shape[0]

    @pl.when(pl.program_id(1) == 0)
    def _():
        s_ref[...] = jnp.zeros_like(s_ref)

    row = lax.broadcasted_iota(jnp.int32, (c, c), 0)
    col = lax.broadcasted_iota(jnp.int32, (c, c), 1)
    lower_incl = row >= col
    eye, masks = _block_masks(c)

    ba = ba_ref[...]
    neg_a = jnp.exp(prm_ref[0:1, :]) * (-LOG2E)
    beta_all = _sigmoid(ba)
    glog = neg_a * _softplus(ba + prm_ref[1:2, :])
    gcum_t = _lane_cumsum(glog.T[0:2 * HEADS, :])
    gcum = jnp.concatenate([gcum_t, jnp.zeros((LANES - 2 * HEADS, c), F32)], axis=0).T

    heads = range(HEADS)
    sls = [slice(i * HEAD_DIM, (i + 1) * HEAD_DIM) for i in heads]
    gcols = [gcum[:, HEADS + i:HEADS + i + 1] for i in heads]
    glasts = [gcum[c - 1:c, HEADS + i:HEADS + i + 1] for i in heads]
    vb, kbg, qd, kend, a_l, attn = [], [], [], [], [], []
    for i in heads:
        qn_b = q_ref[:, sls[i]]
        kn_b = k_ref[:, sls[i]]
        qn = qn_b.astype(F32)
        kn = kn_b.astype(F32)
        beta = beta_all[:, i:i + 1]
        eg = jnp.exp2(gcols[i])
        kb = kn * beta
        vb.append(v_ref[:, sls[i]].astype(F32) * beta)
        kbg.append(kb * eg)
        qd.append(qn * eg)
        kend.append(kn * jnp.exp2(glasts[i] - gcols[i]))
        grow = gcum_t[HEADS + i:HEADS + i + 1, :]
        dec = jnp.exp2(jnp.where(lower_incl, gcols[i] - grow, -jnp.inf))
        gram = _mm_nt(jnp.concatenate([kb.astype(BF16), qn_b], axis=0), kn_b)
        a_l.append((gram[:c, :] * dec).astype(BF16))
        attn.append((gram[c:, :] * dec).astype(BF16))
    ns = [x - eye for x in _unit_lower_inverses(a_l, eye, masks)]
    uws = [jnp.dot(ns[i], jnp.concatenate([vb[i], kbg[i]], axis=1).astype(BF16),
                   preferred_element_type=F32) for i in heads]
    us = [vb[i] + uws[i][:, :HEAD_DIM] for i in heads]
    wq = [jnp.concatenate([kbg[i] + uws[i][:, HEAD_DIM:], qd[i]], axis=0).astype(BF16)
          for i in heads]
    ss = [s_ref[i] for i in heads]
    s_bs = [s.astype(BF16) for s in ss]
    zero_s = jnp.zeros((HEAD_DIM, HEAD_DIM), BF16)
    wss = []
    for i in range(0, HEADS, 2):
        s_pair = jnp.concatenate([jnp.concatenate([s_bs[i], zero_s], axis=1),
                                  jnp.concatenate([zero_s, s_bs[i + 1]], axis=1)], axis=0)
        both = jnp.dot(jnp.concatenate([wq[i], wq[i + 1]], axis=1), s_pair,
                       preferred_element_type=F32)
        wss += [both[:, :HEAD_DIM], both[:, HEAD_DIM:]]
    v_new_b = [(us[i] - wss[i][:c, :]).astype(BF16) for i in heads]
    os_ = [wss[i][c:, :] + jnp.dot(attn[i], v_new_b[i], preferred_element_type=F32)
           for i in heads]
    for i in heads:
        s_ref[i] = ss[i] * jnp.exp2(glasts[i]) + _mm(kend[i].T, v_new_b[i])
    for i in heads:
        o = os_[i]
        on = o * lax.rsqrt(jnp.mean(o * o, axis=-1, keepdims=True) + EPS) * onw_ref[...]
        y_ref[:, sls[i]] = (on * _silu(z_ref[:, sls[i]])).astype(BF16)


def _gdn(qkv, z, ba, prm, onw, batch, seq):
    n = qkv.shape[0]
    c = GDN_CHUNK
    nt = seq // c

    def row_spec(width, col):
        return pl.BlockSpec((c, width), lambda b, t: (b * nt + t, col))

    return pl.pallas_call(
        _gdn_kernel,
        grid=(batch, nt),
        in_specs=[
            row_spec(WIDTH, 0), row_spec(WIDTH, 1), row_spec(WIDTH, 2),
            row_spec(WIDTH, 0),
            row_spec(LANES, 0),
            pl.BlockSpec((SUBLANES, LANES), lambda b, t: (0, 0)),
            pl.BlockSpec((1, HEAD_DIM), lambda b, t: (0, 0)),
        ],
        out_specs=row_spec(WIDTH, 0),
        out_shape=jax.ShapeDtypeStruct((n, WIDTH), BF16),
        scratch_shapes=[pltpu.VMEM((HEADS, HEAD_DIM, HEAD_DIM), F32)],
        compiler_params=pltpu.CompilerParams(
            dimension_semantics=("parallel", "arbitrary"), vmem_limit_bytes=VMEM_LIMIT_BYTES),
        name="gdn",
    )(qkv, qkv, qkv, z, ba, prm, onw)


def _bf16_terms(x):
    hi = x.astype(BF16).astype(F32)
    rest = x - hi
    mid = rest.astype(BF16).astype(F32)
    return hi, mid, (rest - mid).astype(BF16).astype(F32)


def _proj_fox_kernel(y_ref, r_ref, wo_ref, nw_ref, w_ref, fb_ref, qw_ref, kw_ref,
                     h_ref, qt_ref, ko_ref, vt_ref, z_ref, carry_ref):
    tm = r_ref.shape[0]
    cw = MXU_COLS
    per = cw // HEAD_DIM

    @pl.when(pl.program_id(1) == 0)
    def _():
        carry_ref[...] = jnp.zeros_like(carry_ref)

    x = r_ref[...] + jnp.dot(y_ref[...], wo_ref[...], preferred_element_type=F32)
    h_ref[...] = x
    ms = jnp.mean(x * x, axis=-1, keepdims=True)
    hn = (x * lax.rsqrt(ms + EPS) * nw_ref[...]).astype(BF16)

    def chunk(col0):
        return jnp.dot(hn, w_ref[:, col0:col0 + cw], preferred_element_type=F32)

    f = jnp.dot(hn, w_ref[:, 4 * WIDTH:4 * WIDTH + LANES], preferred_element_type=F32) + fb_ref[...]
    logf = -_softplus(-f)
    cum = _lane_cumsum(logf.T[0:HEADS, :]) + carry_ref[...]
    carry_ref[...] = cum[:, tm - 1:tm]
    nc = jnp.concatenate([cum * (-LOG2E), jnp.zeros((LANES - HEADS, tm), F32)], axis=0).T

    lane = lax.broadcasted_iota(jnp.int32, (tm, LANES), 1)
    sub = lax.broadcasted_iota(jnp.int32, (HEAD_DIM, tm), 0)
    q_tail = jnp.where(sub < 3, 1.0, 0.0).astype(BF16)
    q_gain = qw_ref[...] * (HEAD_DIM ** -0.5 * LOG2E)
    k_gain = kw_ref[...]
    for c in range(WIDTH // cw):
        res = chunk(c * cw)
        for h in range(per):
            a = res[:, h * HEAD_DIM:(h + 1) * HEAD_DIM]
            r = lax.rsqrt(jnp.mean(a * a, axis=-1, keepdims=True) + EPS)
            qt_ref[c * per + h, 0:HEAD_DIM, :] = (a * r * q_gain).T.astype(BF16)
            qt_ref[c * per + h, HEAD_DIM:2 * HEAD_DIM, :] = q_tail
    for c in range(WIDTH // cw):
        res = chunk(WIDTH + c * cw)
        for h in range(per):
            head = c * per + h
            a = res[:, h * HEAD_DIM:(h + 1) * HEAD_DIM]
            r = lax.rsqrt(jnp.mean(a * a, axis=-1, keepdims=True) + EPS)
            ko_ref[head, :, 0:HEAD_DIM] = (a * r * k_gain).astype(BF16)
            hi, mid, lo = _bf16_terms(jnp.broadcast_to(nc[:, head:head + 1], (tm, LANES)))
            k_tail = jnp.where(lane == 0, hi, jnp.where(lane == 1, mid, jnp.where(lane == 2, lo, 0.0)))
            ko_ref[head, :, HEAD_DIM:2 * HEAD_DIM] = k_tail.astype(BF16)
    for c in range(WIDTH // cw):
        res = chunk(2 * WIDTH + c * cw)
        for h in range(per):
            vt_ref[c * per + h, 0:HEAD_DIM, :] = res[:, h * HEAD_DIM:(h + 1) * HEAD_DIM].T.astype(BF16)
            vt_ref[c * per + h, HEAD_DIM:V_ROWS, :] = jnp.ones((V_ROWS - HEAD_DIM, tm), BF16)
    for c in range(WIDTH // cw):
        z_ref[:, c * cw:(c + 1) * cw] = chunk(3 * WIDTH + c * cw)


def _proj_fox(y, res, wo, nw, w, fb, qw, kw, batch, seq):
    n, d = res.shape
    d_y = y.shape[1]
    ncol = w.shape[1]
    tm = PROJ_ROWS
    nt = seq // tm

    def row_spec(width):
        return pl.BlockSpec((tm, width), lambda b, t: (b * nt + t, 0))

    def const_spec(shape):
        return pl.BlockSpec(shape, lambda b, t: (0, 0), pipeline_mode=pl.Buffered(1))

    return pl.pallas_call(
        _proj_fox_kernel,
        grid=(batch, nt),
        in_specs=[
            row_spec(d_y), row_spec(d),
            const_spec((d_y, d)), const_spec((1, d)), const_spec((d, ncol)),
            const_spec((1, LANES)), const_spec((1, HEAD_DIM)), const_spec((1, HEAD_DIM)),
        ],
        out_specs=[
            row_spec(d),
            pl.BlockSpec((None, HEADS, 2 * HEAD_DIM, tm), lambda b, t: (b, 0, 0, t)),
            pl.BlockSpec((None, HEADS, tm, 2 * HEAD_DIM), lambda b, t: (b, 0, t, 0)),
            pl.BlockSpec((None, HEADS, V_ROWS, tm), lambda b, t: (b, 0, 0, t)),
            row_spec(WIDTH),
        ],
        out_shape=[
            jax.ShapeDtypeStruct((n, d), F32),
            jax.ShapeDtypeStruct((batch, HEADS, 2 * HEAD_DIM, seq), BF16),
            jax.ShapeDtypeStruct((batch, HEADS, seq, 2 * HEAD_DIM), BF16),
            jax.ShapeDtypeStruct((batch, HEADS, V_ROWS, seq), BF16),
            jax.ShapeDtypeStruct((n, WIDTH), F32),
        ],
        scratch_shapes=[pltpu.VMEM((HEADS, 1), F32)],
        compiler_params=pltpu.CompilerParams(
            dimension_semantics=("parallel", "arbitrary"), vmem_limit_bytes=VMEM_LIMIT_BYTES),
        name="proj_fox",
    )(y, res, wo, nw, w, fb, qw, kw)


def _fox_attn_kernel(qt_ref, k_ref, vt_ref, z_ref, o_ref, m_ref, acc_ref, s_ref, *, blk, nsub):
    qi = pl.program_id(1)
    m_ref[...] = jnp.full_like(m_ref, -jnp.inf)
    acc_ref[...] = jnp.zeros_like(acc_ref)
    pairs = [(d, c) for d in range(nsub) for c in range(nsub)]

    def score(g, half, w):
        d, c = pairs[w]
        start = pl.multiple_of((nsub * g + d) * blk, blk)
        s_ref[half, w] = jnp.dot(k_ref[pl.ds(start, blk), :], qt_ref[:, c * blk:(c + 1) * blk],
                                 preferred_element_type=F32)

    def update(g, half, w, masked):
        d, c = pairs[w]
        start = pl.multiple_of((nsub * g + d) * blk, blk)
        s = s_ref[half, w]
        if masked:
            key = lax.broadcasted_iota(jnp.int32, (blk, blk), 0)
            qry = lax.broadcasted_iota(jnp.int32, (blk, blk), 1)
            s = jnp.where(qry >= key, s, -jnp.inf)
        m_prev = m_ref[c]
        m_new = jnp.maximum(m_prev, jnp.max(s, axis=0, keepdims=True))
        alpha = jnp.exp2(m_prev - m_new)
        p = jnp.exp2(s - jnp.tile(m_new, (blk // SUBLANES, 1)))
        acc_ref[c] = acc_ref[c] * jnp.tile(alpha, (V_ROWS // SUBLANES, 1)) + jnp.dot(
            vt_ref[:, pl.ds(start, blk)], p.astype(BF16), preferred_element_type=F32)
        m_ref[c] = m_new

    slots = range(len(pairs))

    def group(g, half):
        for w in slots:
            score(g + 1, 1 - half, w)
        for w in slots:
            update(g, half, w, False)

    def diagonal(half):
        for w, (d, c) in enumerate(pairs):
            if c >= d:
                update(qi, half, w, c == d)

    for w in slots:
        score(0, 0, w)

    def body(i, carry):
        group(2 * i, 0)
        group(2 * i + 1, 1)
        return carry

    lax.fori_loop(0, qi // 2, body, 0)

    @pl.when(qi % 2 == 0)
    def _():
        diagonal(0)

    @pl.when(qi % 2 == 1)
    def _():
        group(qi - 1, 0)
        diagonal(1)

    for c in range(nsub):
        row_sum = acc_ref[c, HEAD_DIM:HEAD_DIM + SUBLANES, :]
        o = (acc_ref[c, 0:HEAD_DIM, :] / jnp.tile(row_sum, (HEAD_DIM // SUBLANES, 1))).T
        z = z_ref[c * blk:(c + 1) * blk, :]
        o_ref[c * blk:(c + 1) * blk, :] = (o * _silu(z)).astype(BF16)


def _fox_attn(q, k, vt, z, batch, seq):
    blk = ATTN_BLOCK
    nsub = ATTN_SUBBLOCKS
    tq = blk * nsub
    nq = seq // tq
    bh = batch * HEADS
    out_spec = pl.BlockSpec((tq, HEAD_DIM), lambda g, i: ((g // HEADS) * nq + i, g % HEADS))
    return pl.pallas_call(
        functools.partial(_fox_attn_kernel, blk=blk, nsub=nsub),
        grid=(bh, nq),
        in_specs=[
            pl.BlockSpec((None, 2 * HEAD_DIM, tq), lambda g, i: (g, 0, i)),
            pl.BlockSpec((None, seq, 2 * HEAD_DIM), lambda g, i: (g, 0, 0)),
            pl.BlockSpec((None, V_ROWS, seq), lambda g, i: (g, 0, 0)),
            out_spec,
        ],
        out_specs=out_spec,
        out_shape=jax.ShapeDtypeStruct((batch * seq, WIDTH), BF16),
        scratch_shapes=[
            pltpu.VMEM((nsub, SUBLANES, blk), F32),
            pltpu.VMEM((nsub, V_ROWS, blk), F32),
            pltpu.VMEM((2, nsub * nsub, blk, blk), F32),
        ],
        compiler_params=pltpu.CompilerParams(
            dimension_semantics=("parallel", "arbitrary"), vmem_limit_bytes=VMEM_LIMIT_BYTES),
        name="fox_attn",
    )(q, k, vt, z)


def _bf16_padded(w, ncol):
    d, n = w.shape
    return jnp.concatenate([w.astype(BF16), jnp.zeros((d, ncol - n), BF16)], axis=1)


def _gdn_gate_params(a_log, dt_bias):
    prm = jnp.zeros((SUBLANES, LANES), F32)
    prm = prm.at[0, HEADS:2 * HEADS].set(a_log.astype(F32))
    prm = prm.at[1, HEADS:2 * HEADS].set(dt_bias.astype(F32))
    return prm


def kernel(x, a_norm_w, a_w_in, a_conv_w, a_A_log, a_dt_bias, a_o_norm_w, a_w_out, b_norm_w, b_w_in, b_f_bias, b_q_norm_w, b_k_norm_w, b_w_out, final_norm_w):
    batch, seq, d_model = x.shape
    depth = a_norm_w.shape[0] + b_norm_w.shape[0]
    ncol = 4 * WIDTH + LANES
    h = x.reshape(batch * seq, d_model)
    final_nw = final_norm_w.reshape(1, d_model)
    pending = None
    for i in range(depth):
        j = i // N_MIXERS
        last_nw = final_nw if i == depth - 1 else None
        if i % N_MIXERS == 0:
            taps = jnp.pad(a_conv_w[j].astype(F32), ((0, SUBLANES - CONV_WIDTH), (0, 0)))
            qkv, z, ba = _proj_gdn(h, a_norm_w[j].reshape(1, d_model),
                                   _bf16_padded(a_w_in[j], ncol), taps, batch, seq)
            y = _gdn(qkv, z, ba, _gdn_gate_params(a_A_log[j], a_dt_bias[j]),
                     a_o_norm_w[j].reshape(1, HEAD_DIM), batch, seq)
            if i + 1 < depth:
                pending = (y, a_w_out[j].astype(BF16))
            else:
                h = _proj_residual(y, h, a_w_out[j].astype(BF16), last_nw)
        else:
            y_prev, wo_prev = pending
            pending = None
            fb = jnp.pad(b_f_bias[j].astype(F32), (0, LANES - HEADS)).reshape(1, LANES)
            h, qt, k, vt, z = _proj_fox(y_prev, h, wo_prev, b_norm_w[j].reshape(1, d_model),
                                        _bf16_padded(b_w_in[j], ncol), fb,
                                        b_q_norm_w[j].reshape(1, HEAD_DIM),
                                        b_k_norm_w[j].reshape(1, HEAD_DIM), batch, seq)
            bh = batch * HEADS
            y = _fox_attn(qt.reshape(bh, 2 * HEAD_DIM, seq), k.reshape(bh, seq, 2 * HEAD_DIM),
                          vt.reshape(bh, V_ROWS, seq), z, batch, seq)
            h = _proj_residual(y, h, b_w_out[j].astype(BF16), last_nw)
    return h.reshape(batch, seq, d_model)
```

```python
import functools

import jax
import jax.numpy as jnp
from jax import lax
from jax.experimental import pallas as pl
from jax.experimental.pallas import tpu as pltpu

F32 = jnp.float32
BF16 = jnp.bfloat16
EPS = 1e-6

HEADS = 8
HEAD_DIM = 128
WIDTH = HEADS * HEAD_DIM
CONV_WIDTH = 4
N_MIXERS = 2

LANES = 128
SUBLANES = 8
MXU_COLS = 256
VMEM_LIMIT_BYTES = 56 * 1024 * 1024

GDN_CHUNK = 256
GDN_BASE_BLOCK = 16
PROJ_ROWS = 512
GDN_PROJ_ROWS = 512
RESIDUAL_ROWS = 1024
ATTN_BLOCK = 512
ATTN_SUBBLOCKS = 4
ATTN_GROUP = 2
V_ROWS = HEAD_DIM + 16
LOG2E = 1.4426950408889634

NT_DIMS = (((1,), (1,)), ((), ()))


def _sigmoid(x):
    return 0.5 + 0.5 * jnp.tanh(0.5 * x)


def _silu(x):
    h = 0.5 * x
    return h + h * jnp.tanh(h)


def _softplus(x):
    return jnp.maximum(x, 0.0) + jnp.log(1.0 + jnp.exp(-jnp.abs(x)))


def _mm(a, b):
    return jnp.dot(a.astype(BF16), b.astype(BF16), preferred_element_type=F32)


def _mm_nt(a, b):
    return lax.dot_general(a.astype(BF16), b.astype(BF16), NT_DIMS, preferred_element_type=F32)


def _dot_b(a, b):
    return jnp.dot(a, b, preferred_element_type=F32).astype(BF16)


def _normed_rows(x_ref, nw_ref):
    x = x_ref[...]
    ms = jnp.mean(x * x, axis=-1, keepdims=True)
    return (x * lax.rsqrt(ms + EPS) * nw_ref[...]).astype(BF16)


def _lane_cumsum(x):
    n = x.shape[-1]
    lane = lax.broadcasted_iota(jnp.int32, x.shape, x.ndim - 1)
    shift = 1
    while shift < n:
        x = x + jnp.where(lane >= shift, pltpu.roll(x, shift, x.ndim - 1), 0.0)
        shift *= 2
    return x


def _conv_silu_slab(xs_ref, ys_ref, k, taps, tm):
    nph = (tm + SUBLANES) // SUBLANES
    last = CONV_WIDTH - 1
    taps = [jnp.broadcast_to(taps[s:s + 1, :], (SUBLANES, LANES)) for s in range(CONV_WIDTH)]
    ph = [xs_ref[k, pl.ds(r, SUBLANES, stride=nph), :] for r in range(nph)]
    wrapped = {r: pltpu.roll(ph[r], 1, 0) for r in range(nph - last, nph)}
    for r in range(nph):
        acc = ph[r] * taps[last]
        for s in range(1, CONV_WIDTH):
            src = ph[r - s] if r >= s else wrapped[r - s + nph]
            acc = acc + src * taps[last - s]
        ys_ref[k, pl.ds(r, SUBLANES, stride=nph), :] = _silu(acc)


def _proj_gdn_kernel(x_ref, nw_ref, w_ref, cw_ref, qkv_ref, z_ref, ba_ref, tail_ref, xs_ref, ys_ref):
    tm = x_ref.shape[0]
    cw = MXU_COLS
    per = cw // LANES
    assert (tm + SUBLANES) // SUBLANES % 2 == 1

    @pl.when(pl.program_id(1) == 0)
    def _():
        tail_ref[...] = jnp.zeros_like(tail_ref)

    hn = _normed_rows(x_ref, nw_ref)
    n_conv = 3 * WIDTH // cw

    def chunk_dot(c):
        return jnp.dot(hn, w_ref[:, c * cw:(c + 1) * cw], preferred_element_type=F32)

    pre_next = chunk_dot(0)
    for c in range(n_conv):
        pre = pre_next
        if c + 1 < n_conv:
            pre_next = chunk_dot(c + 1)
        for h in range(per):
            slab = c * per + h
            k = (c % 2) * per + h
            lanes = slice(slab * LANES, (slab + 1) * LANES)
            pre_h = pre[:, h * LANES:(h + 1) * LANES]
            xs_ref[k, 0:SUBLANES, :] = tail_ref[slab]
            xs_ref[k, SUBLANES:SUBLANES + tm, :] = pre_h
            tail_ref[slab] = pre_h[tm - SUBLANES:tm, :]
            _conv_silu_slab(xs_ref, ys_ref, k, cw_ref[:, lanes], tm)
            act = ys_ref[k, SUBLANES:SUBLANES + tm, :]
            if slab < 2 * HEADS:
                scale = HEAD_DIM ** -0.5 if slab < HEADS else 1.0
                r = lax.rsqrt(jnp.sum(act * act, axis=-1, keepdims=True) + EPS)
                act = act * (r * scale)
            qkv_ref[:, lanes] = act.astype(BF16)
    for c in range(WIDTH // cw):
        cols = slice(3 * WIDTH + c * cw, 3 * WIDTH + (c + 1) * cw)
        z_ref[:, c * cw:(c + 1) * cw] = jnp.dot(hn, w_ref[:, cols], preferred_element_type=F32)
    ba_ref[...] = jnp.dot(hn, w_ref[:, 4 * WIDTH:4 * WIDTH + LANES], preferred_element_type=F32)


def _proj_gdn(x, nw, w, cw, batch, seq):
    n, d = x.shape
    ncol = w.shape[1]
    tm = GDN_PROJ_ROWS
    nt = seq // tm

    def row_spec(width):
        return pl.BlockSpec((tm, width), lambda b, t: (b * nt + t, 0))

    def const_spec(shape):
        return pl.BlockSpec(shape, lambda b, t: (0, 0), pipeline_mode=pl.Buffered(1))

    return pl.pallas_call(
        _proj_gdn_kernel,
        grid=(batch, nt),
        in_specs=[
            row_spec(d),
            const_spec((1, d)), const_spec((d, ncol)), const_spec((SUBLANES, 3 * WIDTH)),
        ],
        out_specs=[row_spec(3 * WIDTH), row_spec(WIDTH), row_spec(LANES)],
        out_shape=[
            jax.ShapeDtypeStruct((n, 3 * WIDTH), BF16),
            jax.ShapeDtypeStruct((n, WIDTH), F32),
            jax.ShapeDtypeStruct((n, LANES), F32),
        ],
        scratch_shapes=[
            pltpu.VMEM((3 * WIDTH // LANES, SUBLANES, LANES), F32),
            pltpu.VMEM((2 * MXU_COLS // LANES, tm + SUBLANES, LANES), F32),
            pltpu.VMEM((2 * MXU_COLS // LANES, tm + SUBLANES, LANES), F32),
        ],
        compiler_params=pltpu.CompilerParams(
            dimension_semantics=("parallel", "arbitrary"), vmem_limit_bytes=VMEM_LIMIT_BYTES),
        name="proj_gdn",
    )(x, nw, w, cw)


def _proj_residual_kernel(y_ref, r_ref, w_ref, o_ref):
    o_ref[...] = r_ref[...] + jnp.dot(y_ref[...], w_ref[...], preferred_element_type=F32)


def _proj_residual_norm_kernel(y_ref, r_ref, w_ref, nw_ref, o_ref):
    h = r_ref[...] + jnp.dot(y_ref[...], w_ref[...], preferred_element_type=F32)
    ms = jnp.mean(h * h, axis=-1, keepdims=True)
    o_ref[...] = h * lax.rsqrt(ms + EPS) * nw_ref[...]


def _proj_residual(y, res, w, final_nw=None):
    n, d_in = y.shape
    d_out = w.shape[1]
    tm = RESIDUAL_ROWS
    in_specs = [
        pl.BlockSpec((tm, d_in), lambda i: (i, 0)),
        pl.BlockSpec((tm, d_out), lambda i: (i, 0)),
        pl.BlockSpec((d_in, d_out), lambda i: (0, 0)),
    ]
    args = [y, res, w]
    body = _proj_residual_kernel
    if final_nw is not None:
        in_specs.append(pl.BlockSpec((1, d_out), lambda i: (0, 0)))
        args.append(final_nw)
        body = _proj_residual_norm_kernel
    return pl.pallas_call(
        body,
        grid=(n // tm,),
        in_specs=in_specs,
        out_specs=pl.BlockSpec((tm, d_out), lambda i: (i, 0)),
        out_shape=jax.ShapeDtypeStruct((n, d_out), F32),
        compiler_params=pltpu.CompilerParams(
            dimension_semantics=("parallel",), vmem_limit_bytes=VMEM_LIMIT_BYTES),
        name="proj_residual",
    )(*args)


def _block_masks(c):
    row = lax.broadcasted_iota(jnp.int32, (c, c), 0)
    col = lax.broadcasted_iota(jnp.int32, (c, c), 1)
    xr = row ^ col
    eye_f = jnp.where(row == col, 1.0, 0.0)
    eye = eye_f.astype(BF16)
    b = GDN_BASE_BLOCK
    below = jnp.where(xr < b, 1.0, 0.0)
    masks = [(below - eye_f).astype(BF16)]
    while b < c:
        below_next = jnp.where(xr < 2 * b, 1.0, 0.0)
        masks.append((below_next - below).astype(BF16))
        below = below_next
        b *= 2
    return eye, masks


def _unit_lower_inverses(a_list, eye, masks):
    c = eye.shape[0]
    ps = [a * masks[0] for a in a_list]
    xs = [eye - p for p in ps]
    width = 2
    while width < GDN_BASE_BLOCK:
        ps = [_dot_b(p, p) for p in ps]
        xs = [x + _dot_b(x, p) for x, p in zip(xs, ps)]
        width *= 2
    b = GDN_BASE_BLOCK
    for m in masks[1:]:
        starts = range(0, c, 2 * b)
        lows = [jnp.concatenate([x[s + b:s + 2 * b, :] for s in starts], axis=0) for x in xs]
        ys = [_dot_b(lo, a * m) for lo, a in zip(lows, a_list)]
        lows = [lo - _dot_b(y, x) for lo, y, x in zip(lows, ys, xs)]
        xs = [jnp.concatenate([piece for k, s in enumerate(starts)
                               for piece in (x[s:s + b, :], lo[k * b:(k + 1) * b, :])], axis=0)
              for x, lo in zip(xs, lows)]
        b *= 2
    return xs


def _gdn_kernel(q_ref, k_ref, v_ref, z_ref, ba_ref, prm_ref, onw_ref, y_ref, s_ref):
    c = q_ref.shape[0]

    @pl.when(pl.program_id(1) == 0)
    def _():
        s_ref[...] = jnp.zeros_like(s_ref)

    row = lax.broadcasted_iota(jnp.int32, (c, c), 0)
    col = lax.broadcasted_iota(jnp.int32, (c, c), 1)
    lower_incl = row >= col
    eye, masks = _block_masks(c)

    ba = ba_ref[...]
    neg_a = jnp.exp(prm_ref[0:1, :]) * (-LOG2E)
    beta_all = _sigmoid(ba)
    glog = neg_a * _softplus(ba + prm_ref[1:2, :])
    gcum_t = _lane_cumsum(glog.T[0:2 * HEADS, :])
    gcum = jnp.concatenate([gcum_t, jnp.zeros((LANES - 2 * HEADS, c), F32)], axis=0).T

    heads = range(HEADS)
    sls = [slice(i * HEAD_DIM, (i + 1) * HEAD_DIM) for i in heads]
    gcols = [gcum[:, HEADS + i:HEADS + i + 1] for i in heads]
    glasts = [gcum[c - 1:c, HEADS + i:HEADS + i + 1] for i in heads]
    vb, kbg, qd, kend, a_l, attn = [], [], [], [], [], []
    for i in heads:
        qn_b = q_ref[:, sls[i]]
        kn_b = k_ref[:, sls[i]]
        qn = qn_b.astype(F32)
        kn = kn_b.astype(F32)
        beta = beta_all[:, i:i + 1]
        eg = jnp.exp2(gcols[i])
        kb = kn * beta
        vb.append(v_ref[:, sls[i]].astype(F32) * beta)
        kbg.append(kb * eg)
        qd.append(qn * eg)
        kend.append(kn * jnp.exp2(glasts[i] - gcols[i]))
        grow = gcum_t[HEADS + i:HEADS + i + 1, :]
        dec = jnp.exp2(jnp.where(lower_incl, gcols[i] - grow, -jnp.inf))
        gram = _mm_nt(jnp.concatenate([kb.astype(BF16), qn_b], axis=0), kn_b)
        a_l.append((gram[:c, :] * dec).astype(BF16))
        attn.append((gram[c:, :] * dec).astype(BF16))
    ns = [x - eye for x in _unit_lower_inverses(a_l, eye, masks)]
    uws = [jnp.dot(ns[i], jnp.concatenate([vb[i], kbg[i]], axis=1).astype(BF16),
                   preferred_element_type=F32) for i in heads]
    us = [vb[i] + uws[i][:, :HEAD_DIM] for i in heads]
    wq = [jnp.concatenate([kbg[i] + uws[i][:, HEAD_DIM:], qd[i]], axis=0).astype(BF16)
          for i in heads]
    ss = [s_ref[i] for i in heads]
    s_bs = [s.astype(BF16) for s in ss]
    zero_s = jnp.zeros((HEAD_DIM, HEAD_DIM), BF16)
    wss = []
    for i in range(0, HEADS, 2):
        s_pair = jnp.concatenate([jnp.concatenate([s_bs[i], zero_s], axis=1),
                                  jnp.concatenate([zero_s, s_bs[i + 1]], axis=1)], axis=0)
        both = jnp.dot(jnp.concatenate([wq[i], wq[i + 1]], axis=1), s_pair,
                       preferred_element_type=F32)
        wss += [both[:, :HEAD_DIM], both[:, HEAD_DIM:]]
    v_new_b = [(us[i] - wss[i][:c, :]).astype(BF16) for i in heads]
    os_ = [wss[i][c:, :] + jnp.dot(attn[i], v_new_b[i], preferred_element_type=F32)
           for i in heads]
    for i in heads:
        s_ref[i] = ss[i] * jnp.exp2(glasts[i]) + _mm(kend[i].T, v_new_b[i])
    for i in heads:
        o = os_[i]
        on = o * lax.rsqrt(jnp.mean(o * o, axis=-1, keepdims=True) + EPS) * onw_ref[...]
        y_ref[:, sls[i]] = (on * _silu(z_ref[:, sls[i]])).astype(BF16)


def _gdn(qkv, z, ba, prm, onw, batch, seq):
    n = qkv.shape[0]
    c = GDN_CHUNK
    nt = seq // c

    def row_spec(width, col):
        return pl.BlockSpec((c, width), lambda b, t: (b * nt + t, col))

    return pl.pallas_call(
        _gdn_kernel,
        grid=(batch, nt),
        in_specs=[
            row_spec(WIDTH, 0), row_spec(WIDTH, 1), row_spec(WIDTH, 2),
            row_spec(WIDTH, 0),
            row_spec(LANES, 0),
            pl.BlockSpec((SUBLANES, LANES), lambda b, t: (0, 0)),
            pl.BlockSpec((1, HEAD_DIM), lambda b, t: (0, 0)),
        ],
        out_specs=row_spec(WIDTH, 0),
        out_shape=jax.ShapeDtypeStruct((n, WIDTH), BF16),
        scratch_shapes=[pltpu.VMEM((HEADS, HEAD_DIM, HEAD_DIM), F32)],
        compiler_params=pltpu.CompilerParams(
            dimension_semantics=("parallel", "arbitrary"), vmem_limit_bytes=VMEM_LIMIT_BYTES),
        name="gdn",
    )(qkv, qkv, qkv, z, ba, prm, onw)


def _bf16_terms(x):
    hi = x.astype(BF16).astype(F32)
    rest = x - hi
    mid = rest.astype(BF16).astype(F32)
    return hi, mid, (rest - mid).astype(BF16).astype(F32)


def _proj_fox_kernel(y_ref, r_ref, wo_ref, nw_ref, w_ref, fb_ref, qw_ref, kw_ref,
                     h_ref, qt_ref, ko_ref, vt_ref, z_ref, carry_ref):
    tm = r_ref.shape[0]
    cw = MXU_COLS
    per = cw // HEAD_DIM

    @pl.when(pl.program_id(1) == 0)
    def _():
        carry_ref[...] = jnp.zeros_like(carry_ref)

    x = r_ref[...] + jnp.dot(y_ref[...], wo_ref[...], preferred_element_type=F32)
    h_ref[...] = x
    ms = jnp.mean(x * x, axis=-1, keepdims=True)
    hn = (x * lax.rsqrt(ms + EPS) * nw_ref[...]).astype(BF16)

    def chunk(col0):
        return jnp.dot(hn, w_ref[:, col0:col0 + cw], preferred_element_type=F32)

    f = jnp.dot(hn, w_ref[:, 4 * WIDTH:4 * WIDTH + LANES], preferred_element_type=F32) + fb_ref[...]
    logf = -_softplus(-f)
    cum = _lane_cumsum(logf.T[0:HEADS, :]) + carry_ref[...]
    carry_ref[...] = cum[:, tm - 1:tm]
    nc = jnp.concatenate([cum * (-LOG2E), jnp.zeros((LANES - HEADS, tm), F32)], axis=0).T

    lane = lax.broadcasted_iota(jnp.int32, (tm, LANES), 1)
    sub = lax.broadcasted_iota(jnp.int32, (HEAD_DIM, tm), 0)
    q_tail = jnp.where(sub < 3, 1.0, 0.0).astype(BF16)
    q_gain = qw_ref[...] * (HEAD_DIM ** -0.5 * LOG2E)
    k_gain = kw_ref[...]
    for c in range(WIDTH // cw):
        res = chunk(c * cw)
        for h in range(per):
            a = res[:, h * HEAD_DIM:(h + 1) * HEAD_DIM]
            r = lax.rsqrt(jnp.mean(a * a, axis=-1, keepdims=True) + EPS)
            qt_ref[c * per + h, 0:HEAD_DIM, :] = (a * r * q_gain).T.astype(BF16)
            qt_ref[c * per + h, HEAD_DIM:2 * HEAD_DIM, :] = q_tail
    for c in range(WIDTH // cw):
        res = chunk(WIDTH + c * cw)
        for h in range(per):
            head = c * per + h
            a = res[:, h * HEAD_DIM:(h + 1) * HEAD_DIM]
            r = lax.rsqrt(jnp.mean(a * a, axis=-1, keepdims=True) + EPS)
            ko_ref[head, :, 0:HEAD_DIM] = (a * r * k_gain).astype(BF16)
            hi, mid, lo = _bf16_terms(jnp.broadcast_to(nc[:, head:head + 1], (tm, LANES)))
            k_tail = jnp.where(lane == 0, hi, jnp.where(lane == 1, mid, jnp.where(lane == 2, lo, 0.0)))
            ko_ref[head, :, HEAD_DIM:2 * HEAD_DIM] = k_tail.astype(BF16)
    for c in range(WIDTH // cw):
        res = chunk(2 * WIDTH + c * cw)
        for h in range(per):
            vt_ref[c * per + h, 0:HEAD_DIM, :] = res[:, h * HEAD_DIM:(h + 1) * HEAD_DIM].T.astype(BF16)
            vt_ref[c * per + h, HEAD_DIM:V_ROWS, :] = jnp.ones((V_ROWS - HEAD_DIM, tm), BF16)
    for c in range(WIDTH // cw):
        z_ref[:, c * cw:(c + 1) * cw] = chunk(3 * WIDTH + c * cw)


def _proj_fox(y, res, wo, nw, w, fb, qw, kw, batch, seq):
    n, d = res.shape
    d_y = y.shape[1]
    ncol = w.shape[1]
    tm = PROJ_ROWS
    nt = seq // tm

    def row_spec(width):
        return pl.BlockSpec((tm, width), lambda b, t: (b * nt + t, 0))

    def const_spec(shape):
        return pl.BlockSpec(shape, lambda b, t: (0, 0), pipeline_mode=pl.Buffered(1))

    return pl.pallas_call(
        _proj_fox_kernel,
        grid=(batch, nt),
        in_specs=[
            row_spec(d_y), row_spec(d),
            const_spec((d_y, d)), const_spec((1, d)), const_spec((d, ncol)),
            const_spec((1, LANES)), const_spec((1, HEAD_DIM)), const_spec((1, HEAD_DIM)),
        ],
        out_specs=[
            row_spec(d),
            pl.BlockSpec((None, HEADS, 2 * HEAD_DIM, tm), lambda b, t: (b, 0, 0, t)),
            pl.BlockSpec((None, HEADS, tm, 2 * HEAD_DIM), lambda b, t: (b, 0, t, 0)),
            pl.BlockSpec((None, HEADS, V_ROWS, tm), lambda b, t: (b, 0, 0, t)),
            row_spec(WIDTH),
        ],
        out_shape=[
            jax.ShapeDtypeStruct((n, d), F32),
            jax.ShapeDtypeStruct((batch, HEADS, 2 * HEAD_DIM, seq), BF16),
            jax.ShapeDtypeStruct((batch, HEADS, seq, 2 * HEAD_DIM), BF16),
            jax.ShapeDtypeStruct((batch, HEADS, V_ROWS, seq), BF16),
            jax.ShapeDtypeStruct((n, WIDTH), F32),
        ],
        scratch_shapes=[pltpu.VMEM((HEADS, 1), F32)],
        compiler_params=pltpu.CompilerParams(
            dimension_semantics=("parallel", "arbitrary"), vmem_limit_bytes=VMEM_LIMIT_BYTES),
        name="proj_fox",
    )(y, res, wo, nw, w, fb, qw, kw)


def _fox_attn_kernel(qt_ref, k_ref, vt_ref, z_ref, o_ref, m_ref, acc_ref, s_ref, *, blk, nsub, gsz):
    qi = pl.program_id(1)
    m_ref[...] = jnp.full_like(m_ref, -jnp.inf)
    acc_ref[...] = jnp.zeros_like(acc_ref)
    pairs = [(d, c) for d in range(gsz) for c in range(nsub)]
    slots = range(len(pairs))
    diag_groups = nsub // gsz
    assert nsub % gsz == 0 and diag_groups % 2 == 0
    n_main = diag_groups * qi

    def score(g, half, w):
        d, c = pairs[w]
        start = pl.multiple_of((gsz * g + d) * blk, blk)
        s_ref[half, w] = jnp.dot(k_ref[pl.ds(start, blk), :], qt_ref[:, c * blk:(c + 1) * blk],
                                 preferred_element_type=F32)

    def update(g, half, w, masked):
        d, c = pairs[w]
        start = pl.multiple_of((gsz * g + d) * blk, blk)
        s = s_ref[half, w]
        if masked:
            key = lax.broadcasted_iota(jnp.int32, (blk, blk), 0)
            qry = lax.broadcasted_iota(jnp.int32, (blk, blk), 1)
            s = jnp.where(qry >= key, s, -jnp.inf)
        m_prev = m_ref[c]
        m_new = jnp.maximum(m_prev, jnp.max(s, axis=0, keepdims=True))
        alpha = jnp.exp2(m_prev - m_new)
        p = jnp.exp2(s - jnp.tile(m_new, (blk // SUBLANES, 1)))
        acc_ref[c] = acc_ref[c] * jnp.tile(alpha, (V_ROWS // SUBLANES, 1)) + jnp.dot(
            vt_ref[:, pl.ds(start, blk)], p.astype(BF16), preferred_element_type=F32)
        m_ref[c] = m_new

    def group(g, half):
        for w in slots:
            score(g + 1, 1 - half, w)
        for w in slots:
            update(g, half, w, False)

    for w in slots:
        score(0, 0, w)

    def body(i, carry):
        group(2 * i, 0)
        group(2 * i + 1, 1)
        return carry

    lax.fori_loop(0, n_main // 2, body, 0)

    for e in range(diag_groups):
        half = e % 2
        if e + 1 < diag_groups:
            for w, (d, c) in enumerate(pairs):
                if (e + 1) * gsz + d <= c:
                    score(n_main + e + 1, 1 - half, w)
        for w, (d, c) in enumerate(pairs):
            kb = e * gsz + d
            if kb <= c:
                update(n_main + e, half, w, kb == c)

    for c in range(nsub):
        row_sum = acc_ref[c, HEAD_DIM:HEAD_DIM + SUBLANES, :]
        o = (acc_ref[c, 0:HEAD_DIM, :] / jnp.tile(row_sum, (HEAD_DIM // SUBLANES, 1))).T
        z = z_ref[c * blk:(c + 1) * blk, :]
        o_ref[c * blk:(c + 1) * blk, :] = (o * _silu(z)).astype(BF16)


def _fox_attn(q, k, vt, z, batch, seq):
    blk = ATTN_BLOCK
    nsub = ATTN_SUBBLOCKS
    tq = blk * nsub
    nq = seq // tq
    bh = batch * HEADS
    out_spec = pl.BlockSpec((tq, HEAD_DIM), lambda g, i: ((g // HEADS) * nq + i, g % HEADS))
    return pl.pallas_call(
        functools.partial(_fox_attn_kernel, blk=blk, nsub=nsub, gsz=ATTN_GROUP),
        grid=(bh, nq),
        in_specs=[
            pl.BlockSpec((None, 2 * HEAD_DIM, tq), lambda g, i: (g, 0, i)),
            pl.BlockSpec((None, seq, 2 * HEAD_DIM), lambda g, i: (g, 0, 0)),
            pl.BlockSpec((None, V_ROWS, seq), lambda g, i: (g, 0, 0)),
            out_spec,
        ],
        out_specs=out_spec,
        out_shape=jax.ShapeDtypeStruct((batch * seq, WIDTH), BF16),
        scratch_shapes=[
            pltpu.VMEM((nsub, SUBLANES, blk), F32),
            pltpu.VMEM((nsub, V_ROWS, blk), F32),
            pltpu.VMEM((2, ATTN_GROUP * nsub, blk, blk), F32),
        ],
        compiler_params=pltpu.CompilerParams(
            dimension_semantics=("parallel", "arbitrary"), vmem_limit_bytes=VMEM_LIMIT_BYTES),
        name="fox_attn",
    )(q, k, vt, z)


def _bf16_padded(w, ncol):
    d, n = w.shape
    return jnp.concatenate([w.astype(BF16), jnp.zeros((d, ncol - n), BF16)], axis=1)


def _gdn_gate_params(a_log, dt_bias):
    prm = jnp.zeros((SUBLANES, LANES), F32)
    prm = prm.at[0, HEADS:2 * HEADS].set(a_log.astype(F32))
    prm = prm.at[1, HEADS:2 * HEADS].set(dt_bias.astype(F32))
    return prm


def kernel(x, a_norm_w, a_w_in, a_conv_w, a_A_log, a_dt_bias, a_o_norm_w, a_w_out, b_norm_w, b_w_in, b_f_bias, b_q_norm_w, b_k_norm_w, b_w_out, final_norm_w):
    batch, seq, d_model = x.shape
    depth = a_norm_w.shape[0] + b_norm_w.shape[0]
    ncol = 4 * WIDTH + LANES
    h = x.reshape(batch * seq, d_model)
    final_nw = final_norm_w.reshape(1, d_model)
    pending = None
    for i in range(depth):
        j = i // N_MIXERS
        last_nw = final_nw if i == depth - 1 else None
        if i % N_MIXERS == 0:
            taps = jnp.pad(a_conv_w[j].astype(F32), ((0, SUBLANES - CONV_WIDTH), (0, 0)))
            qkv, z, ba = _proj_gdn(h, a_norm_w[j].reshape(1, d_model),
                                   _bf16_padded(a_w_in[j], ncol), taps, batch, seq)
            y = _gdn(qkv, z, ba, _gdn_gate_params(a_A_log[j], a_dt_bias[j]),
                     a_o_norm_w[j].reshape(1, HEAD_DIM), batch, seq)
            if i + 1 < depth:
                pending = (y, a_w_out[j].astype(BF16))
            else:
                h = _proj_residual(y, h, a_w_out[j].astype(BF16), last_nw)
        else:
            y_prev, wo_prev = pending
            pending = None
            fb = jnp.pad(b_f_bias[j].astype(F32), (0, LANES - HEADS)).reshape(1, LANES)
            h, qt, k, vt, z = _proj_fox(y_prev, h, wo_prev, b_norm_w[j].reshape(1, d_model),
                                        _bf16_padded(b_w_in[j], ncol), fb,
                                        b_q_norm_w[j].reshape(1, HEAD_DIM),
                                        b_k_norm_w[j].reshape(1, HEAD_DIM), batch, seq)
            bh = batch * HEADS
            y = _fox_attn(qt.reshape(bh, 2 * HEAD_DIM, seq), k.reshape(bh, seq, 2 * HEAD_DIM),
                          vt.reshape(bh, V_ROWS, seq), z, batch, seq)
            h = _proj_residual(y, h, b_w_out[j].astype(BF16), last_nw)
    return h.reshape(batch, seq, d_model)
```

```python
import functools

import jax
import jax.numpy as jnp
from jax import lax
from jax.experimental import pallas as pl
from jax.experimental.pallas import tpu as pltpu

F32 = jnp.float32
BF16 = jnp.bfloat16
EPS = 1e-6

HEADS = 8
HEAD_DIM = 128
WIDTH = HEADS * HEAD_DIM
CONV_WIDTH = 4
N_MIXERS = 2

LANES = 128
SUBLANES = 8
MXU_COLS = 256
VMEM_LIMIT_BYTES = 56 * 1024 * 1024

GDN_CHUNK = 256
GDN_BASE_BLOCK = 16
PROJ_ROWS = 512
GDN_PROJ_ROWS = 512
RESIDUAL_ROWS = 1024
ATTN_BLOCK = 512
ATTN_SUBBLOCKS = 8
ATTN_GROUP = 1
V_ROWS = HEAD_DIM + 16
LOG2E = 1.4426950408889634

NT_DIMS = (((1,), (1,)), ((), ()))


def _sigmoid(x):
    return 0.5 + 0.5 * jnp.tanh(0.5 * x)


def _silu(x):
    h = 0.5 * x
    return h + h * jnp.tanh(h)


def _softplus(x):
    return jnp.maximum(x, 0.0) + jnp.log(1.0 + jnp.exp(-jnp.abs(x)))


def _mm(a, b):
    return jnp.dot(a.astype(BF16), b.astype(BF16), preferred_element_type=F32)


def _mm_nt(a, b):
    return lax.dot_general(a.astype(BF16), b.astype(BF16), NT_DIMS, preferred_element_type=F32)


def _dot_b(a, b):
    return jnp.dot(a, b, preferred_element_type=F32).astype(BF16)


def _normed_rows(x_ref, nw_ref):
    x = x_ref[...]
    ms = jnp.mean(x * x, axis=-1, keepdims=True)
    return (x * lax.rsqrt(ms + EPS) * nw_ref[...]).astype(BF16)


def _lane_cumsum(x):
    n = x.shape[-1]
    lane = lax.broadcasted_iota(jnp.int32, x.shape, x.ndim - 1)
    shift = 1
    while shift < n:
        x = x + jnp.where(lane >= shift, pltpu.roll(x, shift, x.ndim - 1), 0.0)
        shift *= 2
    return x


def _conv_silu_slab(xs_ref, ys_ref, k, taps, tm):
    nph = (tm + SUBLANES) // SUBLANES
    last = CONV_WIDTH - 1
    taps = [jnp.broadcast_to(taps[s:s + 1, :], (SUBLANES, LANES)) for s in range(CONV_WIDTH)]
    ph = [xs_ref[k, pl.ds(r, SUBLANES, stride=nph), :] for r in range(nph)]
    wrapped = {r: pltpu.roll(ph[r], 1, 0) for r in range(nph - last, nph)}
    for r in range(nph):
        acc = ph[r] * taps[last]
        for s in range(1, CONV_WIDTH):
            src = ph[r - s] if r >= s else wrapped[r - s + nph]
            acc = acc + src * taps[last - s]
        ys_ref[k, pl.ds(r, SUBLANES, stride=nph), :] = _silu(acc)


def _proj_gdn_kernel(x_ref, nw_ref, w_ref, cw_ref, qkv_ref, z_ref, ba_ref, tail_ref, xs_ref, ys_ref):
    tm = x_ref.shape[0]
    cw = MXU_COLS
    per = cw // LANES
    assert (tm + SUBLANES) // SUBLANES % 2 == 1

    @pl.when(pl.program_id(1) == 0)
    def _():
        tail_ref[...] = jnp.zeros_like(tail_ref)

    hn = _normed_rows(x_ref, nw_ref)
    n_conv = 3 * WIDTH // cw

    def chunk_dot(c):
        return jnp.dot(hn, w_ref[:, c * cw:(c + 1) * cw], preferred_element_type=F32)

    pre_next = chunk_dot(0)
    for c in range(n_conv):
        pre = pre_next
        if c + 1 < n_conv:
            pre_next = chunk_dot(c + 1)
        for h in range(per):
            slab = c * per + h
            k = (c % 2) * per + h
            lanes = slice(slab * LANES, (slab + 1) * LANES)
            pre_h = pre[:, h * LANES:(h + 1) * LANES]
            xs_ref[k, 0:SUBLANES, :] = tail_ref[slab]
            xs_ref[k, SUBLANES:SUBLANES + tm, :] = pre_h
            tail_ref[slab] = pre_h[tm - SUBLANES:tm, :]
            _conv_silu_slab(xs_ref, ys_ref, k, cw_ref[:, lanes], tm)
            act = ys_ref[k, SUBLANES:SUBLANES + tm, :]
            if slab < 2 * HEADS:
                scale = HEAD_DIM ** -0.5 if slab < HEADS else 1.0
                r = lax.rsqrt(jnp.sum(act * act, axis=-1, keepdims=True) + EPS)
                act = act * (r * scale)
            qkv_ref[:, lanes] = act.astype(BF16)
    for c in range(WIDTH // cw):
        cols = slice(3 * WIDTH + c * cw, 3 * WIDTH + (c + 1) * cw)
        z_ref[:, c * cw:(c + 1) * cw] = jnp.dot(hn, w_ref[:, cols], preferred_element_type=F32)
    ba_ref[...] = jnp.dot(hn, w_ref[:, 4 * WIDTH:4 * WIDTH + LANES], preferred_element_type=F32)


def _proj_gdn(x, nw, w, cw, batch, seq):
    n, d = x.shape
    ncol = w.shape[1]
    tm = GDN_PROJ_ROWS
    nt = seq // tm

    def row_spec(width):
        return pl.BlockSpec((tm, width), lambda b, t: (b * nt + t, 0))

    def const_spec(shape):
        return pl.BlockSpec(shape, lambda b, t: (0, 0), pipeline_mode=pl.Buffered(1))

    return pl.pallas_call(
        _proj_gdn_kernel,
        grid=(batch, nt),
        in_specs=[
            row_spec(d),
            const_spec((1, d)), const_spec((d, ncol)), const_spec((SUBLANES, 3 * WIDTH)),
        ],
        out_specs=[row_spec(3 * WIDTH), row_spec(WIDTH), row_spec(LANES)],
        out_shape=[
            jax.ShapeDtypeStruct((n, 3 * WIDTH), BF16),
            jax.ShapeDtypeStruct((n, WIDTH), F32),
            jax.ShapeDtypeStruct((n, LANES), F32),
        ],
        scratch_shapes=[
            pltpu.VMEM((3 * WIDTH // LANES, SUBLANES, LANES), F32),
            pltpu.VMEM((2 * MXU_COLS // LANES, tm + SUBLANES, LANES), F32),
            pltpu.VMEM((2 * MXU_COLS // LANES, tm + SUBLANES, LANES), F32),
        ],
        compiler_params=pltpu.CompilerParams(
            dimension_semantics=("parallel", "arbitrary"), vmem_limit_bytes=VMEM_LIMIT_BYTES),
        name="proj_gdn",
    )(x, nw, w, cw)


def _proj_residual_kernel(y_ref, r_ref, w_ref, o_ref):
    o_ref[...] = r_ref[...] + jnp.dot(y_ref[...], w_ref[...], preferred_element_type=F32)


def _proj_residual_norm_kernel(y_ref, r_ref, w_ref, nw_ref, o_ref):
    h = r_ref[...] + jnp.dot(y_ref[...], w_ref[...], preferred_element_type=F32)
    ms = jnp.mean(h * h, axis=-1, keepdims=True)
    o_ref[...] = h * lax.rsqrt(ms + EPS) * nw_ref[...]


def _proj_residual(y, res, w, final_nw=None):
    n, d_in = y.shape
    d_out = w.shape[1]
    tm = RESIDUAL_ROWS
    in_specs = [
        pl.BlockSpec((tm, d_in), lambda i: (i, 0)),
        pl.BlockSpec((tm, d_out), lambda i: (i, 0)),
        pl.BlockSpec((d_in, d_out), lambda i: (0, 0)),
    ]
    args = [y, res, w]
    body = _proj_residual_kernel
    if final_nw is not None:
        in_specs.append(pl.BlockSpec((1, d_out), lambda i: (0, 0)))
        args.append(final_nw)
        body = _proj_residual_norm_kernel
    return pl.pallas_call(
        body,
        grid=(n // tm,),
        in_specs=in_specs,
        out_specs=pl.BlockSpec((tm, d_out), lambda i: (i, 0)),
        out_shape=jax.ShapeDtypeStruct((n, d_out), F32),
        compiler_params=pltpu.CompilerParams(
            dimension_semantics=("parallel",), vmem_limit_bytes=VMEM_LIMIT_BYTES),
        name="proj_residual",
    )(*args)


def _block_masks(c):
    row = lax.broadcasted_iota(jnp.int32, (c, c), 0)
    col = lax.broadcasted_iota(jnp.int32, (c, c), 1)
    xr = row ^ col
    eye_f = jnp.where(row == col, 1.0, 0.0)
    eye = eye_f.astype(BF16)
    b = GDN_BASE_BLOCK
    below = jnp.where(xr < b, 1.0, 0.0)
    masks = [(below - eye_f).astype(BF16)]
    while b < c:
        below_next = jnp.where(xr < 2 * b, 1.0, 0.0)
        masks.append((below_next - below).astype(BF16))
        below = below_next
        b *= 2
    return eye, masks


def _unit_lower_inverses(a_list, eye, masks):
    c = eye.shape[0]
    ps = [a * masks[0] for a in a_list]
    xs = [eye - p for p in ps]
    width = 2
    while width < GDN_BASE_BLOCK:
        ps = [_dot_b(p, p) for p in ps]
        xs = [x + _dot_b(x, p) for x, p in zip(xs, ps)]
        width *= 2
    b = GDN_BASE_BLOCK
    for m in masks[1:]:
        starts = range(0, c, 2 * b)
        lows = [jnp.concatenate([x[s + b:s + 2 * b, :] for s in starts], axis=0) for x in xs]
        ys = [_dot_b(lo, a * m) for lo, a in zip(lows, a_list)]
        lows = [lo - _dot_b(y, x) for lo, y, x in zip(lows, ys, xs)]
        xs = [jnp.concatenate([piece for k, s in enumerate(starts)
                               for piece in (x[s:s + b, :], lo[k * b:(k + 1) * b, :])], axis=0)
              for x, lo in zip(xs, lows)]
        b *= 2
    return xs


def _gdn_kernel(q_ref, k_ref, v_ref, z_ref, ba_ref, prm_ref, onw_ref, y_ref, s_ref):
    c = q_ref.shape[0]

    @pl.when(pl.program_id(1) == 0)
    def _():
        s_ref[...] = jnp.zeros_like(s_ref)

    row = lax.broadcasted_iota(jnp.int32, (c, c), 0)
    col = lax.broadcasted_iota(jnp.int32, (c, c), 1)
    lower_incl = row >= col
    eye, masks = _block_masks(c)

    ba = ba_ref[...]
    neg_a = jnp.exp(prm_ref[0:1, :]) * (-LOG2E)
    beta_all = _sigmoid(ba)
    glog = neg_a * _softplus(ba + prm_ref[1:2, :])
    gcum_t = _lane_cumsum(glog.T[0:2 * HEADS, :])
    gcum = jnp.concatenate([gcum_t, jnp.zeros((LANES - 2 * HEADS, c), F32)], axis=0).T

    heads = range(HEADS)
    sls = [slice(i * HEAD_DIM, (i + 1) * HEAD_DIM) for i in heads]
    gcols = [gcum[:, HEADS + i:HEADS + i + 1] for i in heads]
    glasts = [gcum[c - 1:c, HEADS + i:HEADS + i + 1] for i in heads]
    vb, kbg, qd, kend, a_l, attn = [], [], [], [], [], []
    for i in heads:
        qn_b = q_ref[:, sls[i]]
        kn_b = k_ref[:, sls[i]]
        qn = qn_b.astype(F32)
        kn = kn_b.astype(F32)
        beta = beta_all[:, i:i + 1]
        eg = jnp.exp2(gcols[i])
        kb = kn * beta
        vb.append(v_ref[:, sls[i]].astype(F32) * beta)
        kbg.append(kb * eg)
        qd.append(qn * eg)
        kend.append(kn * jnp.exp2(glasts[i] - gcols[i]))
        grow = gcum_t[HEADS + i:HEADS + i + 1, :]
        dec = jnp.exp2(jnp.where(lower_incl, gcols[i] - grow, -jnp.inf))
        gram = _mm_nt(jnp.concatenate([kb.astype(BF16), qn_b], axis=0), kn_b)
        a_l.append((gram[:c, :] * dec).astype(BF16))
        attn.append((gram[c:, :] * dec).astype(BF16))
    ns = [x - eye for x in _unit_lower_inverses(a_l, eye, masks)]
    uws = [jnp.dot(ns[i], jnp.concatenate([vb[i], kbg[i]], axis=1).astype(BF16),
                   preferred_element_type=F32) for i in heads]
    us = [vb[i] + uws[i][:, :HEAD_DIM] for i in heads]
    wq = [jnp.concatenate([kbg[i] + uws[i][:, HEAD_DIM:], qd[i]], axis=0).astype(BF16)
          for i in heads]
    ss = [s_ref[i] for i in heads]
    s_bs = [s.astype(BF16) for s in ss]
    zero_s = jnp.zeros((HEAD_DIM, HEAD_DIM), BF16)
    wss = []
    for i in range(0, HEADS, 2):
        s_pair = jnp.concatenate([jnp.concatenate([s_bs[i], zero_s], axis=1),
                                  jnp.concatenate([zero_s, s_bs[i + 1]], axis=1)], axis=0)
        both = jnp.dot(jnp.concatenate([wq[i], wq[i + 1]], axis=1), s_pair,
                       preferred_element_type=F32)
        wss += [both[:, :HEAD_DIM], both[:, HEAD_DIM:]]
    v_new_b = [(us[i] - wss[i][:c, :]).astype(BF16) for i in heads]
    os_ = [wss[i][c:, :] + jnp.dot(attn[i], v_new_b[i], preferred_element_type=F32)
           for i in heads]
    for i in heads:
        s_ref[i] = ss[i] * jnp.exp2(glasts[i]) + _mm(kend[i].T, v_new_b[i])
    for i in heads:
        o = os_[i]
        on = o * lax.rsqrt(jnp.mean(o * o, axis=-1, keepdims=True) + EPS) * onw_ref[...]
        y_ref[:, sls[i]] = (on * _silu(z_ref[:, sls[i]])).astype(BF16)


def _gdn(qkv, z, ba, prm, onw, batch, seq):
    n = qkv.shape[0]
    c = GDN_CHUNK
    nt = seq // c

    def row_spec(width, col):
        return pl.BlockSpec((c, width), lambda b, t: (b * nt + t, col))

    return pl.pallas_call(
        _gdn_kernel,
        grid=(batch, nt),
        in_specs=[
            row_spec(WIDTH, 0), row_spec(WIDTH, 1), row_spec(WIDTH, 2),
            row_spec(WIDTH, 0),
            row_spec(LANES, 0),
            pl.BlockSpec((SUBLANES, LANES), lambda b, t: (0, 0)),
            pl.BlockSpec((1, HEAD_DIM), lambda b, t: (0, 0)),
        ],
        out_specs=row_spec(WIDTH, 0),
        out_shape=jax.ShapeDtypeStruct((n, WIDTH), BF16),
        scratch_shapes=[pltpu.VMEM((HEADS, HEAD_DIM, HEAD_DIM), F32)],
        compiler_params=pltpu.CompilerParams(
            dimension_semantics=("parallel", "arbitrary"), vmem_limit_bytes=VMEM_LIMIT_BYTES),
        name="gdn",
    )(qkv, qkv, qkv, z, ba, prm, onw)


def _bf16_terms(x):
    hi = x.astype(BF16).astype(F32)
    rest = x - hi
    mid = rest.astype(BF16).astype(F32)
    return hi, mid, (rest - mid).astype(BF16).astype(F32)


def _proj_fox_kernel(y_ref, r_ref, wo_ref, nw_ref, w_ref, fb_ref, qw_ref, kw_ref,
                     h_ref, qt_ref, ko_ref, vt_ref, z_ref, carry_ref):
    tm = r_ref.shape[0]
    cw = MXU_COLS
    per = cw // HEAD_DIM

    @pl.when(pl.program_id(1) == 0)
    def _():
        carry_ref[...] = jnp.zeros_like(carry_ref)

    x = r_ref[...] + jnp.dot(y_ref[...], wo_ref[...], preferred_element_type=F32)
    h_ref[...] = x
    ms = jnp.mean(x * x, axis=-1, keepdims=True)
    hn = (x * lax.rsqrt(ms + EPS) * nw_ref[...]).astype(BF16)

    def chunk(col0):
        return jnp.dot(hn, w_ref[:, col0:col0 + cw], preferred_element_type=F32)

    f = jnp.dot(hn, w_ref[:, 4 * WIDTH:4 * WIDTH + LANES], preferred_element_type=F32) + fb_ref[...]
    logf = -_softplus(-f)
    cum = _lane_cumsum(logf.T[0:HEADS, :]) + carry_ref[...]
    carry_ref[...] = cum[:, tm - 1:tm]
    nc = jnp.concatenate([cum * (-LOG2E), jnp.zeros((LANES - HEADS, tm), F32)], axis=0).T

    lane = lax.broadcasted_iota(jnp.int32, (tm, LANES), 1)
    sub = lax.broadcasted_iota(jnp.int32, (HEAD_DIM, tm), 0)
    q_tail = jnp.where(sub < 3, 1.0, 0.0).astype(BF16)
    q_gain = qw_ref[...] * (HEAD_DIM ** -0.5 * LOG2E)
    k_gain = kw_ref[...]
    for c in range(WIDTH // cw):
        res = chunk(c * cw)
        for h in range(per):
            a = res[:, h * HEAD_DIM:(h + 1) * HEAD_DIM]
            r = lax.rsqrt(jnp.mean(a * a, axis=-1, keepdims=True) + EPS)
            qt_ref[c * per + h, 0:HEAD_DIM, :] = (a * r * q_gain).T.astype(BF16)
            qt_ref[c * per + h, HEAD_DIM:2 * HEAD_DIM, :] = q_tail
    for c in range(WIDTH // cw):
        res = chunk(WIDTH + c * cw)
        for h in range(per):
            head = c * per + h
            a = res[:, h * HEAD_DIM:(h + 1) * HEAD_DIM]
            r = lax.rsqrt(jnp.mean(a * a, axis=-1, keepdims=True) + EPS)
            ko_ref[head, :, 0:HEAD_DIM] = (a * r * k_gain).astype(BF16)
            hi, mid, lo = _bf16_terms(jnp.broadcast_to(nc[:, head:head + 1], (tm, LANES)))
            k_tail = jnp.where(lane == 0, hi, jnp.where(lane == 1, mid, jnp.where(lane == 2, lo, 0.0)))
            ko_ref[head, :, HEAD_DIM:2 * HEAD_DIM] = k_tail.astype(BF16)
    for c in range(WIDTH // cw):
        res = chunk(2 * WIDTH + c * cw)
        for h in range(per):
            vt_ref[c * per + h, 0:HEAD_DIM, :] = res[:, h * HEAD_DIM:(h + 1) * HEAD_DIM].T.astype(BF16)
            vt_ref[c * per + h, HEAD_DIM:V_ROWS, :] = jnp.ones((V_ROWS - HEAD_DIM, tm), BF16)
    for c in range(WIDTH // cw):
        z_ref[:, c * cw:(c + 1) * cw] = chunk(3 * WIDTH + c * cw)


def _proj_fox(y, res, wo, nw, w, fb, qw, kw, batch, seq):
    n, d = res.shape
    d_y = y.shape[1]
    ncol = w.shape[1]
    tm = PROJ_ROWS
    nt = seq // tm

    def row_spec(width):
        return pl.BlockSpec((tm, width), lambda b, t: (b * nt + t, 0))

    def const_spec(shape):
        return pl.BlockSpec(shape, lambda b, t: (0, 0), pipeline_mode=pl.Buffered(1))

    return pl.pallas_call(
        _proj_fox_kernel,
        grid=(batch, nt),
        in_specs=[
            row_spec(d_y), row_spec(d),
            const_spec((d_y, d)), const_spec((1, d)), const_spec((d, ncol)),
            const_spec((1, LANES)), const_spec((1, HEAD_DIM)), const_spec((1, HEAD_DIM)),
        ],
        out_specs=[
            row_spec(d),
            pl.BlockSpec((None, HEADS, 2 * HEAD_DIM, tm), lambda b, t: (b, 0, 0, t)),
            pl.BlockSpec((None, HEADS, tm, 2 * HEAD_DIM), lambda b, t: (b, 0, t, 0)),
            pl.BlockSpec((None, HEADS, V_ROWS, tm), lambda b, t: (b, 0, 0, t)),
            row_spec(WIDTH),
        ],
        out_shape=[
            jax.ShapeDtypeStruct((n, d), F32),
            jax.ShapeDtypeStruct((batch, HEADS, 2 * HEAD_DIM, seq), BF16),
            jax.ShapeDtypeStruct((batch, HEADS, seq, 2 * HEAD_DIM), BF16),
            jax.ShapeDtypeStruct((batch, HEADS, V_ROWS, seq), BF16),
            jax.ShapeDtypeStruct((n, WIDTH), F32),
        ],
        scratch_shapes=[pltpu.VMEM((HEADS, 1), F32)],
        compiler_params=pltpu.CompilerParams(
            dimension_semantics=("parallel", "arbitrary"), vmem_limit_bytes=VMEM_LIMIT_BYTES),
        name="proj_fox",
    )(y, res, wo, nw, w, fb, qw, kw)


def _fox_attn_kernel(qt_ref, k_ref, vt_ref, z_ref, o_ref, m_ref, acc_ref, s_ref, *, blk, nsub, gsz):
    qi = pl.program_id(1)
    m_ref[...] = jnp.full_like(m_ref, -jnp.inf)
    acc_ref[...] = jnp.zeros_like(acc_ref)
    pairs = [(d, c) for d in range(gsz) for c in range(nsub)]
    slots = range(len(pairs))
    diag_groups = nsub // gsz
    assert nsub % gsz == 0 and diag_groups % 2 == 0
    n_main = diag_groups * qi

    def score(g, half, w):
        d, c = pairs[w]
        start = pl.multiple_of((gsz * g + d) * blk, blk)
        s_ref[half, w] = jnp.dot(k_ref[pl.ds(start, blk), :], qt_ref[:, c * blk:(c + 1) * blk],
                                 preferred_element_type=F32)

    def update(g, half, w, masked):
        d, c = pairs[w]
        start = pl.multiple_of((gsz * g + d) * blk, blk)
        s = s_ref[half, w]
        if masked:
            key = lax.broadcasted_iota(jnp.int32, (blk, blk), 0)
            qry = lax.broadcasted_iota(jnp.int32, (blk, blk), 1)
            s = jnp.where(qry >= key, s, -jnp.inf)
        m_prev = m_ref[c]
        m_new = jnp.maximum(m_prev, jnp.max(s, axis=0, keepdims=True))
        alpha = jnp.exp2(m_prev - m_new)
        p = jnp.exp2(s - jnp.tile(m_new, (blk // SUBLANES, 1)))
        acc_ref[c] = acc_ref[c] * jnp.tile(alpha, (V_ROWS // SUBLANES, 1)) + jnp.dot(
            vt_ref[:, pl.ds(start, blk)], p.astype(BF16), preferred_element_type=F32)
        m_ref[c] = m_new

    def group(g, half):
        for w in slots:
            score(g + 1, 1 - half, w)
        for w in slots:
            update(g, half, w, False)

    for w in slots:
        score(0, 0, w)

    def body(i, carry):
        group(2 * i, 0)
        group(2 * i + 1, 1)
        return carry

    lax.fori_loop(0, n_main // 2, body, 0)

    for e in range(diag_groups):
        half = e % 2
        if e + 1 < diag_groups:
            for w, (d, c) in enumerate(pairs):
                if (e + 1) * gsz + d <= c:
                    score(n_main + e + 1, 1 - half, w)
        for w, (d, c) in enumerate(pairs):
            kb = e * gsz + d
            if kb <= c:
                update(n_main + e, half, w, kb == c)

    for c in range(nsub):
        row_sum = acc_ref[c, HEAD_DIM:HEAD_DIM + SUBLANES, :]
        o = (acc_ref[c, 0:HEAD_DIM, :] / jnp.tile(row_sum, (HEAD_DIM // SUBLANES, 1))).T
        z = z_ref[c * blk:(c + 1) * blk, :]
        o_ref[c * blk:(c + 1) * blk, :] = (o * _silu(z)).astype(BF16)


def _fox_attn(q, k, vt, z, batch, seq):
    blk = ATTN_BLOCK
    nsub = ATTN_SUBBLOCKS
    tq = blk * nsub
    nq = seq // tq
    bh = batch * HEADS
    out_spec = pl.BlockSpec((tq, HEAD_DIM), lambda g, i: ((g // HEADS) * nq + i, g % HEADS))
    return pl.pallas_call(
        functools.partial(_fox_attn_kernel, blk=blk, nsub=nsub, gsz=ATTN_GROUP),
        grid=(bh, nq),
        in_specs=[
            pl.BlockSpec((None, 2 * HEAD_DIM, tq), lambda g, i: (g, 0, i)),
            pl.BlockSpec((None, seq, 2 * HEAD_DIM), lambda g, i: (g, 0, 0)),
            pl.BlockSpec((None, V_ROWS, seq), lambda g, i: (g, 0, 0)),
            out_spec,
        ],
        out_specs=out_spec,
        out_shape=jax.ShapeDtypeStruct((batch * seq, WIDTH), BF16),
        scratch_shapes=[
            pltpu.VMEM((nsub, SUBLANES, blk), F32),
            pltpu.VMEM((nsub, V_ROWS, blk), F32),
            pltpu.VMEM((2, ATTN_GROUP * nsub, blk, blk), F32),
        ],
        compiler_params=pltpu.CompilerParams(
            dimension_semantics=("parallel", "arbitrary"), vmem_limit_bytes=VMEM_LIMIT_BYTES),
        name="fox_attn",
    )(q, k, vt, z)


def _bf16_padded(w, ncol):
    d, n = w.shape
    return jnp.concatenate([w.astype(BF16), jnp.zeros((d, ncol - n), BF16)], axis=1)


def _gdn_gate_params(a_log, dt_bias):
    prm = jnp.zeros((SUBLANES, LANES), F32)
    prm = prm.at[0, HEADS:2 * HEADS].set(a_log.astype(F32))
    prm = prm.at[1, HEADS:2 * HEADS].set(dt_bias.astype(F32))
    return prm


def kernel(x, a_norm_w, a_w_in, a_conv_w, a_A_log, a_dt_bias, a_o_norm_w, a_w_out, b_norm_w, b_w_in, b_f_bias, b_q_norm_w, b_k_norm_w, b_w_out, final_norm_w):
    batch, seq, d_model = x.shape
    depth = a_norm_w.shape[0] + b_norm_w.shape[0]
    ncol = 4 * WIDTH + LANES
    h = x.reshape(batch * seq, d_model)
    final_nw = final_norm_w.reshape(1, d_model)
    pending = None
    for i in range(depth):
        j = i // N_MIXERS
        last_nw = final_nw if i == depth - 1 else None
        if i % N_MIXERS == 0:
            taps = jnp.pad(a_conv_w[j].astype(F32), ((0, SUBLANES - CONV_WIDTH), (0, 0)))
            qkv, z, ba = _proj_gdn(h, a_norm_w[j].reshape(1, d_model),
                                   _bf16_padded(a_w_in[j], ncol), taps, batch, seq)
            y = _gdn(qkv, z, ba, _gdn_gate_params(a_A_log[j], a_dt_bias[j]),
                     a_o_norm_w[j].reshape(1, HEAD_DIM), batch, seq)
            if i + 1 < depth:
                pending = (y, a_w_out[j].astype(BF16))
            else:
                h = _proj_residual(y, h, a_w_out[j].astype(BF16), last_nw)
        else:
            y_prev, wo_prev = pending
            pending = None
            fb = jnp.pad(b_f_bias[j].astype(F32), (0, LANES - HEADS)).reshape(1, LANES)
            h, qt, k, vt, z = _proj_fox(y_prev, h, wo_prev, b_norm_w[j].reshape(1, d_model),
                                        _bf16_padded(b_w_in[j], ncol), fb,
                                        b_q_norm_w[j].reshape(1, HEAD_DIM),
                                        b_k_norm_w[j].reshape(1, HEAD_DIM), batch, seq)
            bh = batch * HEADS
            y = _fox_attn(qt.reshape(bh, 2 * HEAD_DIM, seq), k.reshape(bh, seq, 2 * HEAD_DIM),
                          vt.reshape(bh, V_ROWS, seq), z, batch, seq)
            h = _proj_residual(y, h, b_w_out[j].astype(BF16), last_nw)
    return h.reshape(batch, seq, d_model)
```

```python
import functools

import jax
import jax.numpy as jnp
from jax import lax
from jax.experimental import pallas as pl
from jax.experimental.pallas import tpu as pltpu

F32 = jnp.float32
BF16 = jnp.bfloat16
EPS = 1e-6

HEADS = 8
HEAD_DIM = 128
WIDTH = HEADS * HEAD_DIM
CONV_WIDTH = 4
N_MIXERS = 2

LANES = 128
SUBLANES = 8
MXU_COLS = 256
VMEM_LIMIT_BYTES = 56 * 1024 * 1024

GDN_CHUNK = 256
GDN_BASE_BLOCK = 16
PROJ_ROWS = 512
GDN_PROJ_ROWS = 512
RESIDUAL_ROWS = 1024
ATTN_BLOCK = 512
ATTN_SUBBLOCKS = 8
ATTN_GROUP = 1
V_ROWS = HEAD_DIM + 16
LOG2E = 1.4426950408889634

NT_DIMS = (((1,), (1,)), ((), ()))


def _sigmoid(x):
    return 0.5 + 0.5 * jnp.tanh(0.5 * x)


def _silu(x):
    h = 0.5 * x
    return h + h * jnp.tanh(h)


def _softplus(x):
    return jnp.maximum(x, 0.0) + jnp.log(1.0 + jnp.exp(-jnp.abs(x)))


def _mm(a, b):
    return jnp.dot(a.astype(BF16), b.astype(BF16), preferred_element_type=F32)


def _mm_nt(a, b):
    return lax.dot_general(a.astype(BF16), b.astype(BF16), NT_DIMS, preferred_element_type=F32)


def _dot_b(a, b):
    return jnp.dot(a, b, preferred_element_type=F32).astype(BF16)


def _normed_rows(x_ref, nw_ref):
    x = x_ref[...]
    ms = jnp.mean(x * x, axis=-1, keepdims=True)
    return (x * lax.rsqrt(ms + EPS) * nw_ref[...]).astype(BF16)


def _tail_columns(hn, w_ref):
    tail = jnp.dot(hn, w_ref[:, 4 * WIDTH:], preferred_element_type=F32)
    return jnp.concatenate(
        [tail, jnp.zeros((tail.shape[0], LANES - tail.shape[1]), F32)], axis=1)


def _lane_cumsum(x):
    n = x.shape[-1]
    lane = lax.broadcasted_iota(jnp.int32, x.shape, x.ndim - 1)
    shift = 1
    while shift < n:
        x = x + jnp.where(lane >= shift, pltpu.roll(x, shift, x.ndim - 1), 0.0)
        shift *= 2
    return x


def _conv_silu_slab(xs_ref, ys_ref, k, taps, tm):
    nph = (tm + SUBLANES) // SUBLANES
    last = CONV_WIDTH - 1
    taps = [jnp.broadcast_to(taps[s:s + 1, :], (SUBLANES, LANES)) for s in range(CONV_WIDTH)]
    ph = [xs_ref[k, pl.ds(r, SUBLANES, stride=nph), :] for r in range(nph)]
    wrapped = {r: pltpu.roll(ph[r], 1, 0) for r in range(nph - last, nph)}
    for r in range(nph):
        acc = ph[r] * taps[last]
        for s in range(1, CONV_WIDTH):
            src = ph[r - s] if r >= s else wrapped[r - s + nph]
            acc = acc + src * taps[last - s]
        ys_ref[k, pl.ds(r, SUBLANES, stride=nph), :] = _silu(acc)


def _proj_gdn_kernel(x_ref, nw_ref, w_ref, cw_ref, qkv_ref, z_ref, ba_ref, tail_ref, xs_ref, ys_ref):
    tm = x_ref.shape[0]
    cw = MXU_COLS
    per = cw // LANES
    assert (tm + SUBLANES) // SUBLANES % 2 == 1

    @pl.when(pl.program_id(1) == 0)
    def _():
        tail_ref[...] = jnp.zeros_like(tail_ref)

    hn = _normed_rows(x_ref, nw_ref)
    n_conv = 3 * WIDTH // cw

    def chunk_dot(c):
        return jnp.dot(hn, w_ref[:, c * cw:(c + 1) * cw], preferred_element_type=F32)

    pre_next = chunk_dot(0)
    for c in range(n_conv):
        pre = pre_next
        if c + 1 < n_conv:
            pre_next = chunk_dot(c + 1)
        for h in range(per):
            slab = c * per + h
            k = (c % 2) * per + h
            lanes = slice(slab * LANES, (slab + 1) * LANES)
            pre_h = pre[:, h * LANES:(h + 1) * LANES]
            xs_ref[k, 0:SUBLANES, :] = tail_ref[slab]
            xs_ref[k, SUBLANES:SUBLANES + tm, :] = pre_h
            tail_ref[slab] = pre_h[tm - SUBLANES:tm, :]
            _conv_silu_slab(xs_ref, ys_ref, k, cw_ref[:, lanes], tm)
            act = ys_ref[k, SUBLANES:SUBLANES + tm, :]
            if slab < 2 * HEADS:
                scale = HEAD_DIM ** -0.5 if slab < HEADS else 1.0
                r = lax.rsqrt(jnp.sum(act * act, axis=-1, keepdims=True) + EPS)
                act = act * (r * scale)
            qkv_ref[:, lanes] = act.astype(BF16)
    for c in range(WIDTH // cw):
        cols = slice(3 * WIDTH + c * cw, 3 * WIDTH + (c + 1) * cw)
        z_ref[:, c * cw:(c + 1) * cw] = jnp.dot(hn, w_ref[:, cols], preferred_element_type=F32)
    ba_ref[...] = _tail_columns(hn, w_ref)


def _proj_gdn(x, nw, w, cw, batch, seq):
    n, d = x.shape
    ncol = w.shape[1]
    tm = GDN_PROJ_ROWS
    nt = seq // tm

    def row_spec(width):
        return pl.BlockSpec((tm, width), lambda b, t: (b * nt + t, 0))

    def const_spec(shape):
        return pl.BlockSpec(shape, lambda b, t: (0, 0), pipeline_mode=pl.Buffered(1))

    return pl.pallas_call(
        _proj_gdn_kernel,
        grid=(batch, nt),
        in_specs=[
            row_spec(d),
            const_spec((1, d)), const_spec((d, ncol)), const_spec((SUBLANES, 3 * WIDTH)),
        ],
        out_specs=[row_spec(3 * WIDTH), row_spec(WIDTH), row_spec(LANES)],
        out_shape=[
            jax.ShapeDtypeStruct((n, 3 * WIDTH), BF16),
            jax.ShapeDtypeStruct((n, WIDTH), F32),
            jax.ShapeDtypeStruct((n, LANES), F32),
        ],
        scratch_shapes=[
            pltpu.VMEM((3 * WIDTH // LANES, SUBLANES, LANES), F32),
            pltpu.VMEM((2 * MXU_COLS // LANES, tm + SUBLANES, LANES), F32),
            pltpu.VMEM((2 * MXU_COLS // LANES, tm + SUBLANES, LANES), F32),
        ],
        compiler_params=pltpu.CompilerParams(
            dimension_semantics=("parallel", "arbitrary"), vmem_limit_bytes=VMEM_LIMIT_BYTES),
        name="proj_gdn",
    )(x, nw, w, cw)


def _proj_residual_kernel(y_ref, r_ref, w_ref, o_ref):
    o_ref[...] = r_ref[...] + jnp.dot(y_ref[...], w_ref[...], preferred_element_type=F32)


def _proj_residual_norm_kernel(y_ref, r_ref, w_ref, nw_ref, o_ref):
    h = r_ref[...] + jnp.dot(y_ref[...], w_ref[...], preferred_element_type=F32)
    ms = jnp.mean(h * h, axis=-1, keepdims=True)
    o_ref[...] = h * lax.rsqrt(ms + EPS) * nw_ref[...]


def _proj_residual(y, res, w, final_nw=None):
    n, d_in = y.shape
    d_out = w.shape[1]
    tm = RESIDUAL_ROWS
    in_specs = [
        pl.BlockSpec((tm, d_in), lambda i: (i, 0)),
        pl.BlockSpec((tm, d_out), lambda i: (i, 0)),
        pl.BlockSpec((d_in, d_out), lambda i: (0, 0)),
    ]
    args = [y, res, w]
    body = _proj_residual_kernel
    if final_nw is not None:
        in_specs.append(pl.BlockSpec((1, d_out), lambda i: (0, 0)))
        args.append(final_nw)
        body = _proj_residual_norm_kernel
    return pl.pallas_call(
        body,
        grid=(n // tm,),
        in_specs=in_specs,
        out_specs=pl.BlockSpec((tm, d_out), lambda i: (i, 0)),
        out_shape=jax.ShapeDtypeStruct((n, d_out), F32),
        compiler_params=pltpu.CompilerParams(
            dimension_semantics=("parallel",), vmem_limit_bytes=VMEM_LIMIT_BYTES),
        name="proj_residual",
    )(*args)


def _block_masks(c):
    row = lax.broadcasted_iota(jnp.int32, (c, c), 0)
    col = lax.broadcasted_iota(jnp.int32, (c, c), 1)
    xr = row ^ col
    eye_f = jnp.where(row == col, 1.0, 0.0)
    eye = eye_f.astype(BF16)
    b = GDN_BASE_BLOCK
    below = jnp.where(xr < b, 1.0, 0.0)
    masks = [(below - eye_f).astype(BF16)]
    while b < c:
        below_next = jnp.where(xr < 2 * b, 1.0, 0.0)
        masks.append((below_next - below).astype(BF16))
        below = below_next
        b *= 2
    return eye, masks


def _unit_lower_inverses(a_list, eye, masks):
    c = eye.shape[0]
    ps = [a * masks[0] for a in a_list]
    xs = [eye - p for p in ps]
    width = 2
    while width < GDN_BASE_BLOCK:
        ps = [_dot_b(p, p) for p in ps]
        xs = [x + _dot_b(x, p) for x, p in zip(xs, ps)]
        width *= 2
    b = GDN_BASE_BLOCK
    for m in masks[1:]:
        starts = range(0, c, 2 * b)
        lows = [jnp.concatenate([x[s + b:s + 2 * b, :] for s in starts], axis=0) for x in xs]
        ys = [_dot_b(lo, a * m) for lo, a in zip(lows, a_list)]
        lows = [lo - _dot_b(y, x) for lo, y, x in zip(lows, ys, xs)]
        xs = [jnp.concatenate([piece for k, s in enumerate(starts)
                               for piece in (x[s:s + b, :], lo[k * b:(k + 1) * b, :])], axis=0)
              for x, lo in zip(xs, lows)]
        b *= 2
    return xs


def _gdn_kernel(q_ref, k_ref, v_ref, z_ref, ba_ref, prm_ref, onw_ref, y_ref, s_ref):
    c = q_ref.shape[0]

    @pl.when(pl.program_id(1) == 0)
    def _():
        s_ref[...] = jnp.zeros_like(s_ref)

    row = lax.broadcasted_iota(jnp.int32, (c, c), 0)
    col = lax.broadcasted_iota(jnp.int32, (c, c), 1)
    lower_incl = row >= col
    eye, masks = _block_masks(c)

    ba = ba_ref[...]
    neg_a = jnp.exp(prm_ref[0:1, :]) * (-LOG2E)
    beta_all = _sigmoid(ba)
    glog = neg_a * _softplus(ba + prm_ref[1:2, :])
    gcum_t = _lane_cumsum(glog.T[0:2 * HEADS, :])
    gcum = jnp.concatenate([gcum_t, jnp.zeros((LANES - 2 * HEADS, c), F32)], axis=0).T

    heads = range(HEADS)
    sls = [slice(i * HEAD_DIM, (i + 1) * HEAD_DIM) for i in heads]
    gcols = [gcum[:, HEADS + i:HEADS + i + 1] for i in heads]
    glasts = [gcum[c - 1:c, HEADS + i:HEADS + i + 1] for i in heads]
    vb, kbg, qd, kend, a_l, attn = [], [], [], [], [], []
    for i in heads:
        qn_b = q_ref[:, sls[i]]
        kn_b = k_ref[:, sls[i]]
        qn = qn_b.astype(F32)
        kn = kn_b.astype(F32)
        beta = beta_all[:, i:i + 1]
        eg = jnp.exp2(gcols[i])
        kb = kn * beta
        vb.append(v_ref[:, sls[i]].astype(F32) * beta)
        kbg.append(kb * eg)
        qd.append(qn * eg)
        kend.append(kn * jnp.exp2(glasts[i] - gcols[i]))
        grow = gcum_t[HEADS + i:HEADS + i + 1, :]
        dec = jnp.exp2(jnp.where(lower_incl, gcols[i] - grow, -jnp.inf))
        gram = _mm_nt(jnp.concatenate([kb.astype(BF16), qn_b], axis=0), kn_b)
        a_l.append((gram[:c, :] * dec).astype(BF16))
        attn.append((gram[c:, :] * dec).astype(BF16))
    ns = [x - eye for x in _unit_lower_inverses(a_l, eye, masks)]
    uws = [jnp.dot(ns[i], jnp.concatenate([vb[i], kbg[i]], axis=1).astype(BF16),
                   preferred_element_type=F32) for i in heads]
    us = [vb[i] + uws[i][:, :HEAD_DIM] for i in heads]
    wq = [jnp.concatenate([kbg[i] + uws[i][:, HEAD_DIM:], qd[i]], axis=0).astype(BF16)
          for i in heads]
    ss = [s_ref[i] for i in heads]
    s_bs = [s.astype(BF16) for s in ss]
    zero_s = jnp.zeros((HEAD_DIM, HEAD_DIM), BF16)
    wss = []
    for i in range(0, HEADS, 2):
        s_pair = jnp.concatenate([jnp.concatenate([s_bs[i], zero_s], axis=1),
                                  jnp.concatenate([zero_s, s_bs[i + 1]], axis=1)], axis=0)
        both = jnp.dot(jnp.concatenate([wq[i], wq[i + 1]], axis=1), s_pair,
                       preferred_element_type=F32)
        wss += [both[:, :HEAD_DIM], both[:, HEAD_DIM:]]
    v_new_b = [(us[i] - wss[i][:c, :]).astype(BF16) for i in heads]
    os_ = [wss[i][c:, :] + jnp.dot(attn[i], v_new_b[i], preferred_element_type=F32)
           for i in heads]
    for i in heads:
        s_ref[i] = ss[i] * jnp.exp2(glasts[i]) + _mm(kend[i].T, v_new_b[i])
    for i in heads:
        o = os_[i]
        on = o * lax.rsqrt(jnp.mean(o * o, axis=-1, keepdims=True) + EPS) * onw_ref[...]
        y_ref[:, sls[i]] = (on * _silu(z_ref[:, sls[i]])).astype(BF16)


def _gdn(qkv, z, ba, prm, onw, batch, seq):
    n = qkv.shape[0]
    c = GDN_CHUNK
    nt = seq // c

    def row_spec(width, col):
        return pl.BlockSpec((c, width), lambda b, t: (b * nt + t, col))

    return pl.pallas_call(
        _gdn_kernel,
        grid=(batch, nt),
        in_specs=[
            row_spec(WIDTH, 0), row_spec(WIDTH, 1), row_spec(WIDTH, 2),
            row_spec(WIDTH, 0),
            row_spec(LANES, 0),
            pl.BlockSpec((SUBLANES, LANES), lambda b, t: (0, 0)),
            pl.BlockSpec((1, HEAD_DIM), lambda b, t: (0, 0)),
        ],
        out_specs=row_spec(WIDTH, 0),
        out_shape=jax.ShapeDtypeStruct((n, WIDTH), BF16),
        scratch_shapes=[pltpu.VMEM((HEADS, HEAD_DIM, HEAD_DIM), F32)],
        compiler_params=pltpu.CompilerParams(
            dimension_semantics=("parallel", "arbitrary"), vmem_limit_bytes=VMEM_LIMIT_BYTES),
        name="gdn",
    )(qkv, qkv, qkv, z, ba, prm, onw)


def _bf16_terms(x):
    hi = x.astype(BF16).astype(F32)
    rest = x - hi
    mid = rest.astype(BF16).astype(F32)
    return hi, mid, (rest - mid).astype(BF16).astype(F32)


def _proj_fox_kernel(y_ref, r_ref, wo_ref, nw_ref, w_ref, fb_ref, qw_ref, kw_ref,
                     h_ref, qt_ref, ko_ref, vt_ref, z_ref, carry_ref):
    tm = r_ref.shape[0]
    cw = MXU_COLS
    per = cw // HEAD_DIM

    @pl.when(pl.program_id(1) == 0)
    def _():
        carry_ref[...] = jnp.zeros_like(carry_ref)

    x = r_ref[...] + jnp.dot(y_ref[...], wo_ref[...], preferred_element_type=F32)
    h_ref[...] = x
    ms = jnp.mean(x * x, axis=-1, keepdims=True)
    hn = (x * lax.rsqrt(ms + EPS) * nw_ref[...]).astype(BF16)

    def chunk(col0):
        return jnp.dot(hn, w_ref[:, col0:col0 + cw], preferred_element_type=F32)

    f = _tail_columns(hn, w_ref) + fb_ref[...]
    logf = -_softplus(-f)
    cum = _lane_cumsum(logf.T[0:HEADS, :]) + carry_ref[...]
    carry_ref[...] = cum[:, tm - 1:tm]
    nc = jnp.concatenate([cum * (-LOG2E), jnp.zeros((LANES - HEADS, tm), F32)], axis=0).T

    lane = lax.broadcasted_iota(jnp.int32, (tm, LANES), 1)
    sub = lax.broadcasted_iota(jnp.int32, (HEAD_DIM, tm), 0)
    q_tail = jnp.where(sub < 3, 1.0, 0.0).astype(BF16)
    q_gain = qw_ref[...] * (HEAD_DIM ** -0.5 * LOG2E)
    k_gain = kw_ref[...]
    for c in range(WIDTH // cw):
        res = chunk(c * cw)
        for h in range(per):
            a = res[:, h * HEAD_DIM:(h + 1) * HEAD_DIM]
            r = lax.rsqrt(jnp.mean(a * a, axis=-1, keepdims=True) + EPS)
            qt_ref[c * per + h, 0:HEAD_DIM, :] = (a * r * q_gain).T.astype(BF16)
            qt_ref[c * per + h, HEAD_DIM:2 * HEAD_DIM, :] = q_tail
    for c in range(WIDTH // cw):
        res = chunk(WIDTH + c * cw)
        for h in range(per):
            head = c * per + h
            a = res[:, h * HEAD_DIM:(h + 1) * HEAD_DIM]
            r = lax.rsqrt(jnp.mean(a * a, axis=-1, keepdims=True) + EPS)
            ko_ref[head, :, 0:HEAD_DIM] = (a * r * k_gain).astype(BF16)
            hi, mid, lo = _bf16_terms(jnp.broadcast_to(nc[:, head:head + 1], (tm, LANES)))
            k_tail = jnp.where(lane == 0, hi, jnp.where(lane == 1, mid, jnp.where(lane == 2, lo, 0.0)))
            ko_ref[head, :, HEAD_DIM:2 * HEAD_DIM] = k_tail.astype(BF16)
    for c in range(WIDTH // cw):
        res = chunk(2 * WIDTH + c * cw)
        for h in range(per):
            vt_ref[c * per + h, 0:HEAD_DIM, :] = res[:, h * HEAD_DIM:(h + 1) * HEAD_DIM].T.astype(BF16)
            vt_ref[c * per + h, HEAD_DIM:V_ROWS, :] = jnp.ones((V_ROWS - HEAD_DIM, tm), BF16)
    for c in range(WIDTH // cw):
        z_ref[:, c * cw:(c + 1) * cw] = chunk(3 * WIDTH + c * cw)


def _proj_fox(y, res, wo, nw, w, fb, qw, kw, batch, seq):
    n, d = res.shape
    d_y = y.shape[1]
    ncol = w.shape[1]
    tm = PROJ_ROWS
    nt = seq // tm

    def row_spec(width):
        return pl.BlockSpec((tm, width), lambda b, t: (b * nt + t, 0))

    def const_spec(shape):
        return pl.BlockSpec(shape, lambda b, t: (0, 0), pipeline_mode=pl.Buffered(1))

    return pl.pallas_call(
        _proj_fox_kernel,
        grid=(batch, nt),
        in_specs=[
            row_spec(d_y), row_spec(d),
            const_spec((d_y, d)), const_spec((1, d)), const_spec((d, ncol)),
            const_spec((1, LANES)), const_spec((1, HEAD_DIM)), const_spec((1, HEAD_DIM)),
        ],
        out_specs=[
            row_spec(d),
            pl.BlockSpec((None, HEADS, 2 * HEAD_DIM, tm), lambda b, t: (b, 0, 0, t)),
            pl.BlockSpec((None, HEADS, tm, 2 * HEAD_DIM), lambda b, t: (b, 0, t, 0)),
            pl.BlockSpec((None, HEADS, V_ROWS, tm), lambda b, t: (b, 0, 0, t)),
            row_spec(WIDTH),
        ],
        out_shape=[
            jax.ShapeDtypeStruct((n, d), F32),
            jax.ShapeDtypeStruct((batch, HEADS, 2 * HEAD_DIM, seq), BF16),
            jax.ShapeDtypeStruct((batch, HEADS, seq, 2 * HEAD_DIM), BF16),
            jax.ShapeDtypeStruct((batch, HEADS, V_ROWS, seq), BF16),
            jax.ShapeDtypeStruct((n, WIDTH), F32),
        ],
        scratch_shapes=[pltpu.VMEM((HEADS, 1), F32)],
        compiler_params=pltpu.CompilerParams(
            dimension_semantics=("parallel", "arbitrary"), vmem_limit_bytes=VMEM_LIMIT_BYTES),
        name="proj_fox",
    )(y, res, wo, nw, w, fb, qw, kw)


def _fox_attn_kernel(qt_ref, k_ref, vt_ref, z_ref, o_ref, m_ref, acc_ref, s_ref, *, blk, nsub, gsz):
    qi = pl.program_id(1)
    m_ref[...] = jnp.full_like(m_ref, -jnp.inf)
    acc_ref[...] = jnp.zeros_like(acc_ref)
    pairs = [(d, c) for d in range(gsz) for c in range(nsub)]
    slots = range(len(pairs))
    diag_groups = nsub // gsz
    assert nsub % gsz == 0 and diag_groups % 2 == 0
    n_main = diag_groups * qi

    def score(g, half, w):
        d, c = pairs[w]
        start = pl.multiple_of((gsz * g + d) * blk, blk)
        s_ref[half, w] = jnp.dot(k_ref[pl.ds(start, blk), :], qt_ref[:, c * blk:(c + 1) * blk],
                                 preferred_element_type=F32)

    def update(g, half, w, masked):
        d, c = pairs[w]
        start = pl.multiple_of((gsz * g + d) * blk, blk)
        s = s_ref[half, w]
        if masked:
            key = lax.broadcasted_iota(jnp.int32, (blk, blk), 0)
            qry = lax.broadcasted_iota(jnp.int32, (blk, blk), 1)
            s = jnp.where(qry >= key, s, -jnp.inf)
        m_prev = m_ref[c]
        m_new = jnp.maximum(m_prev, jnp.max(s, axis=0, keepdims=True))
        alpha = jnp.exp2(m_prev - m_new)
        p = jnp.exp2(s - jnp.tile(m_new, (blk // SUBLANES, 1)))
        acc_ref[c] = acc_ref[c] * jnp.tile(alpha, (V_ROWS // SUBLANES, 1)) + jnp.dot(
            vt_ref[:, pl.ds(start, blk)], p.astype(BF16), preferred_element_type=F32)
        m_ref[c] = m_new

    def group(g, half):
        for w in slots:
            score(g + 1, 1 - half, w)
        for w in slots:
            update(g, half, w, False)

    for w in slots:
        score(0, 0, w)

    def body(i, carry):
        group(2 * i, 0)
        group(2 * i + 1, 1)
        return carry

    lax.fori_loop(0, n_main // 2, body, 0)

    for e in range(diag_groups):
        half = e % 2
        if e + 1 < diag_groups:
            for w, (d, c) in enumerate(pairs):
                if (e + 1) * gsz + d <= c:
                    score(n_main + e + 1, 1 - half, w)
        for w, (d, c) in enumerate(pairs):
            kb = e * gsz + d
            if kb <= c:
                update(n_main + e, half, w, kb == c)

    for c in range(nsub):
        row_sum = acc_ref[c, HEAD_DIM:HEAD_DIM + SUBLANES, :]
        o = (acc_ref[c, 0:HEAD_DIM, :] / jnp.tile(row_sum, (HEAD_DIM // SUBLANES, 1))).T
        z = z_ref[c * blk:(c + 1) * blk, :]
        o_ref[c * blk:(c + 1) * blk, :] = (o * _silu(z)).astype(BF16)


def _fox_attn(q, k, vt, z, batch, seq):
    blk = ATTN_BLOCK
    nsub = ATTN_SUBBLOCKS
    tq = blk * nsub
    nq = seq // tq
    bh = batch * HEADS
    out_spec = pl.BlockSpec((tq, HEAD_DIM), lambda g, i: ((g // HEADS) * nq + i, g % HEADS))
    return pl.pallas_call(
        functools.partial(_fox_attn_kernel, blk=blk, nsub=nsub, gsz=ATTN_GROUP),
        grid=(bh, nq),
        in_specs=[
            pl.BlockSpec((None, 2 * HEAD_DIM, tq), lambda g, i: (g, 0, i)),
            pl.BlockSpec((None, seq, 2 * HEAD_DIM), lambda g, i: (g, 0, 0)),
            pl.BlockSpec((None, V_ROWS, seq), lambda g, i: (g, 0, 0)),
            out_spec,
        ],
        out_specs=out_spec,
        out_shape=jax.ShapeDtypeStruct((batch * seq, WIDTH), BF16),
        scratch_shapes=[
            pltpu.VMEM((nsub, SUBLANES, blk), F32),
            pltpu.VMEM((nsub, V_ROWS, blk), F32),
            pltpu.VMEM((2, ATTN_GROUP * nsub, blk, blk), F32),
        ],
        compiler_params=pltpu.CompilerParams(
            dimension_semantics=("parallel", "arbitrary"), vmem_limit_bytes=VMEM_LIMIT_BYTES),
        name="fox_attn",
    )(q, k, vt, z)


def _gdn_gate_params(a_log, dt_bias):
    prm = jnp.zeros((SUBLANES, LANES), F32)
    prm = prm.at[0, HEADS:2 * HEADS].set(a_log.astype(F32))
    prm = prm.at[1, HEADS:2 * HEADS].set(dt_bias.astype(F32))
    return prm


def kernel(x, a_norm_w, a_w_in, a_conv_w, a_A_log, a_dt_bias, a_o_norm_w, a_w_out, b_norm_w, b_w_in, b_f_bias, b_q_norm_w, b_k_norm_w, b_w_out, final_norm_w):
    batch, seq, d_model = x.shape
    depth = a_norm_w.shape[0] + b_norm_w.shape[0]
    h = x.reshape(batch * seq, d_model)
    final_nw = final_norm_w.reshape(1, d_model)
    pending = None
    for i in range(depth):
        j = i // N_MIXERS
        last_nw = final_nw if i == depth - 1 else None
        if i % N_MIXERS == 0:
            taps = jnp.pad(a_conv_w[j].astype(F32), ((0, SUBLANES - CONV_WIDTH), (0, 0)))
            qkv, z, ba = _proj_gdn(h, a_norm_w[j].reshape(1, d_model),
                                   a_w_in[j].astype(BF16), taps, batch, seq)
            y = _gdn(qkv, z, ba, _gdn_gate_params(a_A_log[j], a_dt_bias[j]),
                     a_o_norm_w[j].reshape(1, HEAD_DIM), batch, seq)
            if i + 1 < depth:
                pending = (y, a_w_out[j].astype(BF16))
            else:
                h = _proj_residual(y, h, a_w_out[j].astype(BF16), last_nw)
        else:
            y_prev, wo_prev = pending
            pending = None
            fb = jnp.pad(b_f_bias[j].astype(F32), (0, LANES - HEADS)).reshape(1, LANES)
            h, qt, k, vt, z = _proj_fox(y_prev, h, wo_prev, b_norm_w[j].reshape(1, d_model),
                                        b_w_in[j].astype(BF16), fb,
                                        b_q_norm_w[j].reshape(1, HEAD_DIM),
                                        b_k_norm_w[j].reshape(1, HEAD_DIM), batch, seq)
            bh = batch * HEADS
            y = _fox_attn(qt.reshape(bh, 2 * HEAD_DIM, seq), k.reshape(bh, seq, 2 * HEAD_DIM),
                          vt.reshape(bh, V_ROWS, seq), z, batch, seq)
            h = _proj_residual(y, h, b_w_out[j].astype(BF16), last_nw)
    return h.reshape(batch, seq, d_model)
```

```python
import functools

import jax
import jax.numpy as jnp
from jax import lax
from jax.experimental import pallas as pl
from jax.experimental.pallas import tpu as pltpu

F32 = jnp.float32
BF16 = jnp.bfloat16
EPS = 1e-6

HEADS = 8
HEAD_DIM = 128
WIDTH = HEADS * HEAD_DIM
CONV_WIDTH = 4
N_MIXERS = 2

LANES = 128
SUBLANES = 8
MXU_COLS = 256
VMEM_LIMIT_BYTES = 56 * 1024 * 1024

GDN_CHUNK = 256
GDN_BASE_BLOCK = 8
PROJ_ROWS = 512
GDN_PROJ_ROWS = 512
RESIDUAL_ROWS = 1024
ATTN_BLOCK = 512
ATTN_SUBBLOCKS = 8
ATTN_GROUP = 1
V_ROWS = HEAD_DIM + 16
LOG2E = 1.4426950408889634

NT_DIMS = (((1,), (1,)), ((), ()))


def _sigmoid(x):
    return 0.5 + 0.5 * jnp.tanh(0.5 * x)


def _silu(x):
    h = 0.5 * x
    return h + h * jnp.tanh(h)


def _softplus(x):
    return jnp.maximum(x, 0.0) + jnp.log(1.0 + jnp.exp(-jnp.abs(x)))


def _mm(a, b):
    return jnp.dot(a.astype(BF16), b.astype(BF16), preferred_element_type=F32)


def _mm_nt(a, b):
    return lax.dot_general(a.astype(BF16), b.astype(BF16), NT_DIMS, preferred_element_type=F32)


def _dot_b(a, b):
    return jnp.dot(a, b, preferred_element_type=F32).astype(BF16)


def _normed_rows(x_ref, nw_ref):
    x = x_ref[...]
    ms = jnp.mean(x * x, axis=-1, keepdims=True)
    return (x * lax.rsqrt(ms + EPS) * nw_ref[...]).astype(BF16)


def _tail_columns(hn, w_ref):
    tail = jnp.dot(hn, w_ref[:, 4 * WIDTH:], preferred_element_type=F32)
    return jnp.concatenate(
        [tail, jnp.zeros((tail.shape[0], LANES - tail.shape[1]), F32)], axis=1)


def _lane_cumsum(x):
    n = x.shape[-1]
    lane = lax.broadcasted_iota(jnp.int32, x.shape, x.ndim - 1)
    shift = 1
    while shift < n:
        x = x + jnp.where(lane >= shift, pltpu.roll(x, shift, x.ndim - 1), 0.0)
        shift *= 2
    return x


def _conv_silu_slab(xs_ref, ys_ref, k, taps, tm):
    nph = (tm + SUBLANES) // SUBLANES
    last = CONV_WIDTH - 1
    taps = [jnp.broadcast_to(taps[s:s + 1, :], (SUBLANES, LANES)) for s in range(CONV_WIDTH)]
    ph = [xs_ref[k, pl.ds(r, SUBLANES, stride=nph), :] for r in range(nph)]
    wrapped = {r: pltpu.roll(ph[r], 1, 0) for r in range(nph - last, nph)}
    for r in range(nph):
        acc = ph[r] * taps[last]
        for s in range(1, CONV_WIDTH):
            src = ph[r - s] if r >= s else wrapped[r - s + nph]
            acc = acc + src * taps[last - s]
        ys_ref[k, pl.ds(r, SUBLANES, stride=nph), :] = _silu(acc)


def _proj_gdn_kernel(x_ref, nw_ref, w_ref, cw_ref, qkv_ref, z_ref, ba_ref, tail_ref, xs_ref, ys_ref):
    tm = x_ref.shape[0]
    cw = MXU_COLS
    per = cw // LANES
    assert (tm + SUBLANES) // SUBLANES % 2 == 1

    @pl.when(pl.program_id(1) == 0)
    def _():
        tail_ref[...] = jnp.zeros_like(tail_ref)

    hn = _normed_rows(x_ref, nw_ref)
    n_conv = 3 * WIDTH // cw

    def chunk_dot(c):
        return jnp.dot(hn, w_ref[:, c * cw:(c + 1) * cw], preferred_element_type=F32)

    pre_next = chunk_dot(0)
    for c in range(n_conv):
        pre = pre_next
        if c + 1 < n_conv:
            pre_next = chunk_dot(c + 1)
        for h in range(per):
            slab = c * per + h
            k = (c % 2) * per + h
            lanes = slice(slab * LANES, (slab + 1) * LANES)
            pre_h = pre[:, h * LANES:(h + 1) * LANES]
            xs_ref[k, 0:SUBLANES, :] = tail_ref[slab]
            xs_ref[k, SUBLANES:SUBLANES + tm, :] = pre_h
            tail_ref[slab] = pre_h[tm - SUBLANES:tm, :]
            _conv_silu_slab(xs_ref, ys_ref, k, cw_ref[:, lanes], tm)
            act = ys_ref[k, SUBLANES:SUBLANES + tm, :]
            if slab < 2 * HEADS:
                scale = HEAD_DIM ** -0.5 if slab < HEADS else 1.0
                r = lax.rsqrt(jnp.sum(act * act, axis=-1, keepdims=True) + EPS)
                act = act * (r * scale)
            qkv_ref[:, lanes] = act.astype(BF16)
    for c in range(WIDTH // cw):
        cols = slice(3 * WIDTH + c * cw, 3 * WIDTH + (c + 1) * cw)
        z_ref[:, c * cw:(c + 1) * cw] = jnp.dot(hn, w_ref[:, cols], preferred_element_type=F32)
    ba_ref[...] = _tail_columns(hn, w_ref)


def _proj_gdn(x, nw, w, cw, batch, seq):
    n, d = x.shape
    ncol = w.shape[1]
    tm = GDN_PROJ_ROWS
    nt = seq // tm

    def row_spec(width):
        return pl.BlockSpec((tm, width), lambda b, t: (b * nt + t, 0))

    def const_spec(shape):
        return pl.BlockSpec(shape, lambda b, t: (0, 0), pipeline_mode=pl.Buffered(1))

    return pl.pallas_call(
        _proj_gdn_kernel,
        grid=(batch, nt),
        in_specs=[
            row_spec(d),
            const_spec((1, d)), const_spec((d, ncol)), const_spec((SUBLANES, 3 * WIDTH)),
        ],
        out_specs=[row_spec(3 * WIDTH), row_spec(WIDTH), row_spec(LANES)],
        out_shape=[
            jax.ShapeDtypeStruct((n, 3 * WIDTH), BF16),
            jax.ShapeDtypeStruct((n, WIDTH), F32),
            jax.ShapeDtypeStruct((n, LANES), F32),
        ],
        scratch_shapes=[
            pltpu.VMEM((3 * WIDTH // LANES, SUBLANES, LANES), F32),
            pltpu.VMEM((2 * MXU_COLS // LANES, tm + SUBLANES, LANES), F32),
            pltpu.VMEM((2 * MXU_COLS // LANES, tm + SUBLANES, LANES), F32),
        ],
        compiler_params=pltpu.CompilerParams(
            dimension_semantics=("parallel", "arbitrary"), vmem_limit_bytes=VMEM_LIMIT_BYTES),
        name="proj_gdn",
    )(x, nw, w, cw)


def _proj_residual_kernel(y_ref, r_ref, w_ref, o_ref):
    o_ref[...] = r_ref[...] + jnp.dot(y_ref[...], w_ref[...], preferred_element_type=F32)


def _proj_residual_norm_kernel(y_ref, r_ref, w_ref, nw_ref, o_ref):
    h = r_ref[...] + jnp.dot(y_ref[...], w_ref[...], preferred_element_type=F32)
    ms = jnp.mean(h * h, axis=-1, keepdims=True)
    o_ref[...] = h * lax.rsqrt(ms + EPS) * nw_ref[...]


def _proj_residual(y, res, w, final_nw=None):
    n, d_in = y.shape
    d_out = w.shape[1]
    tm = RESIDUAL_ROWS
    in_specs = [
        pl.BlockSpec((tm, d_in), lambda i: (i, 0)),
        pl.BlockSpec((tm, d_out), lambda i: (i, 0)),
        pl.BlockSpec((d_in, d_out), lambda i: (0, 0)),
    ]
    args = [y, res, w]
    body = _proj_residual_kernel
    if final_nw is not None:
        in_specs.append(pl.BlockSpec((1, d_out), lambda i: (0, 0)))
        args.append(final_nw)
        body = _proj_residual_norm_kernel
    return pl.pallas_call(
        body,
        grid=(n // tm,),
        in_specs=in_specs,
        out_specs=pl.BlockSpec((tm, d_out), lambda i: (i, 0)),
        out_shape=jax.ShapeDtypeStruct((n, d_out), F32),
        compiler_params=pltpu.CompilerParams(
            dimension_semantics=("parallel",), vmem_limit_bytes=VMEM_LIMIT_BYTES),
        name="proj_residual",
    )(*args)


def _block_masks(c):
    row = lax.broadcasted_iota(jnp.int32, (c, c), 0)
    col = lax.broadcasted_iota(jnp.int32, (c, c), 1)
    xr = row ^ col
    eye_f = jnp.where(row == col, 1.0, 0.0)
    eye = eye_f.astype(BF16)
    b = GDN_BASE_BLOCK
    below = jnp.where(xr < b, 1.0, 0.0)
    masks = [(below - eye_f).astype(BF16)]
    while b < c:
        below_next = jnp.where(xr < 2 * b, 1.0, 0.0)
        masks.append((below_next - below).astype(BF16))
        below = below_next
        b *= 2
    return eye, masks


def _unit_lower_inverses(a_list, eye, masks):
    c = eye.shape[0]
    ps = [a * masks[0] for a in a_list]
    xs = [eye - p for p in ps]
    width = 2
    while width < GDN_BASE_BLOCK:
        ps = [_dot_b(p, p) for p in ps]
        xs = [x + _dot_b(x, p) for x, p in zip(xs, ps)]
        width *= 2
    b = GDN_BASE_BLOCK
    for m in masks[1:]:
        starts = range(0, c, 2 * b)
        lows = [jnp.concatenate([x[s + b:s + 2 * b, :] for s in starts], axis=0) for x in xs]
        ys = [_dot_b(lo, a * m) for lo, a in zip(lows, a_list)]
        lows = [lo - _dot_b(y, x) for lo, y, x in zip(lows, ys, xs)]
        xs = [jnp.concatenate([piece for k, s in enumerate(starts)
                               for piece in (x[s:s + b, :], lo[k * b:(k + 1) * b, :])], axis=0)
              for x, lo in zip(xs, lows)]
        b *= 2
    return xs


def _gdn_kernel(q_ref, k_ref, v_ref, z_ref, ba_ref, prm_ref, onw_ref, y_ref, s_ref):
    c = q_ref.shape[0]

    @pl.when(pl.program_id(1) == 0)
    def _():
        s_ref[...] = jnp.zeros_like(s_ref)

    row = lax.broadcasted_iota(jnp.int32, (c, c), 0)
    col = lax.broadcasted_iota(jnp.int32, (c, c), 1)
    lower_incl = row >= col
    eye, masks = _block_masks(c)

    ba = ba_ref[...]
    neg_a = jnp.exp(prm_ref[0:1, :]) * (-LOG2E)
    beta_all = _sigmoid(ba)
    glog = neg_a * _softplus(ba + prm_ref[1:2, :])
    gcum_t = _lane_cumsum(glog.T[0:2 * HEADS, :])
    gcum = jnp.concatenate([gcum_t, jnp.zeros((LANES - 2 * HEADS, c), F32)], axis=0).T

    heads = range(HEADS)
    sls = [slice(i * HEAD_DIM, (i + 1) * HEAD_DIM) for i in heads]
    gcols = [gcum[:, HEADS + i:HEADS + i + 1] for i in heads]
    glasts = [gcum[c - 1:c, HEADS + i:HEADS + i + 1] for i in heads]
    vb, kbg, qd, kend, a_l, attn = [], [], [], [], [], []
    for i in heads:
        qn_b = q_ref[:, sls[i]]
        kn_b = k_ref[:, sls[i]]
        qn = qn_b.astype(F32)
        kn = kn_b.astype(F32)
        beta = beta_all[:, i:i + 1]
        eg = jnp.exp2(gcols[i])
        kb = kn * beta
        vb.append(v_ref[:, sls[i]].astype(F32) * beta)
        kbg.append(kb * eg)
        qd.append(qn * eg)
        kend.append(kn * jnp.exp2(glasts[i] - gcols[i]))
        grow = gcum_t[HEADS + i:HEADS + i + 1, :]
        dec = jnp.exp2(jnp.where(lower_incl, gcols[i] - grow, -jnp.inf))
        gram = _mm_nt(jnp.concatenate([kb.astype(BF16), qn_b], axis=0), kn_b)
        a_l.append((gram[:c, :] * dec).astype(BF16))
        attn.append((gram[c:, :] * dec).astype(BF16))
    ns = [x - eye for x in _unit_lower_inverses(a_l, eye, masks)]
    uws = [jnp.dot(ns[i], jnp.concatenate([vb[i], kbg[i]], axis=1).astype(BF16),
                   preferred_element_type=F32) for i in heads]
    us = [vb[i] + uws[i][:, :HEAD_DIM] for i in heads]
    wq = [jnp.concatenate([kbg[i] + uws[i][:, HEAD_DIM:], qd[i]], axis=0).astype(BF16)
          for i in heads]
    ss = [s_ref[i] for i in heads]
    s_bs = [s.astype(BF16) for s in ss]
    zero_s = jnp.zeros((HEAD_DIM, HEAD_DIM), BF16)
    wss = []
    for i in range(0, HEADS, 2):
        s_pair = jnp.concatenate([jnp.concatenate([s_bs[i], zero_s], axis=1),
                                  jnp.concatenate([zero_s, s_bs[i + 1]], axis=1)], axis=0)
        both = jnp.dot(jnp.concatenate([wq[i], wq[i + 1]], axis=1), s_pair,
                       preferred_element_type=F32)
        wss += [both[:, :HEAD_DIM], both[:, HEAD_DIM:]]
    v_new_b = [(us[i] - wss[i][:c, :]).astype(BF16) for i in heads]
    os_ = [wss[i][c:, :] + jnp.dot(attn[i], v_new_b[i], preferred_element_type=F32)
           for i in heads]
    for i in heads:
        s_ref[i] = ss[i] * jnp.exp2(glasts[i]) + _mm(kend[i].T, v_new_b[i])
    for i in heads:
        o = os_[i]
        on = o * lax.rsqrt(jnp.mean(o * o, axis=-1, keepdims=True) + EPS) * onw_ref[...]
        y_ref[:, sls[i]] = (on * _silu(z_ref[:, sls[i]])).astype(BF16)


def _gdn(qkv, z, ba, prm, onw, batch, seq):
    n = qkv.shape[0]
    c = GDN_CHUNK
    nt = seq // c

    def row_spec(width, col):
        return pl.BlockSpec((c, width), lambda b, t: (b * nt + t, col))

    return pl.pallas_call(
        _gdn_kernel,
        grid=(batch, nt),
        in_specs=[
            row_spec(WIDTH, 0), row_spec(WIDTH, 1), row_spec(WIDTH, 2),
            row_spec(WIDTH, 0),
            row_spec(LANES, 0),
            pl.BlockSpec((SUBLANES, LANES), lambda b, t: (0, 0)),
            pl.BlockSpec((1, HEAD_DIM), lambda b, t: (0, 0)),
        ],
        out_specs=row_spec(WIDTH, 0),
        out_shape=jax.ShapeDtypeStruct((n, WIDTH), BF16),
        scratch_shapes=[pltpu.VMEM((HEADS, HEAD_DIM, HEAD_DIM), F32)],
        compiler_params=pltpu.CompilerParams(
            dimension_semantics=("parallel", "arbitrary"), vmem_limit_bytes=VMEM_LIMIT_BYTES),
        name="gdn",
    )(qkv, qkv, qkv, z, ba, prm, onw)


def _bf16_terms(x):
    hi = x.astype(BF16).astype(F32)
    rest = x - hi
    mid = rest.astype(BF16).astype(F32)
    return hi, mid, (rest - mid).astype(BF16).astype(F32)


def _proj_fox_kernel(y_ref, r_ref, wo_ref, nw_ref, w_ref, fb_ref, qw_ref, kw_ref,
                     h_ref, qt_ref, ko_ref, vt_ref, z_ref, carry_ref):
    tm = r_ref.shape[0]
    cw = MXU_COLS
    per = cw // HEAD_DIM

    @pl.when(pl.program_id(1) == 0)
    def _():
        carry_ref[...] = jnp.zeros_like(carry_ref)

    x = r_ref[...] + jnp.dot(y_ref[...], wo_ref[...], preferred_element_type=F32)
    h_ref[...] = x
    ms = jnp.mean(x * x, axis=-1, keepdims=True)
    hn = (x * lax.rsqrt(ms + EPS) * nw_ref[...]).astype(BF16)

    def chunk(col0):
        return jnp.dot(hn, w_ref[:, col0:col0 + cw], preferred_element_type=F32)

    f = _tail_columns(hn, w_ref) + fb_ref[...]
    logf = -_softplus(-f)
    cum = _lane_cumsum(logf.T[0:HEADS, :]) + carry_ref[...]
    carry_ref[...] = cum[:, tm - 1:tm]
    nc = jnp.concatenate([cum * (-LOG2E), jnp.zeros((LANES - HEADS, tm), F32)], axis=0).T

    lane = lax.broadcasted_iota(jnp.int32, (tm, LANES), 1)
    sub = lax.broadcasted_iota(jnp.int32, (HEAD_DIM, tm), 0)
    q_tail = jnp.where(sub < 3, 1.0, 0.0).astype(BF16)
    q_gain = qw_ref[...] * (HEAD_DIM ** -0.5 * LOG2E)
    k_gain = kw_ref[...]
    for c in range(WIDTH // cw):
        res = chunk(c * cw)
        for h in range(per):
            a = res[:, h * HEAD_DIM:(h + 1) * HEAD_DIM]
            r = lax.rsqrt(jnp.mean(a * a, axis=-1, keepdims=True) + EPS)
            qt_ref[c * per + h, 0:HEAD_DIM, :] = (a * r * q_gain).T.astype(BF16)
            qt_ref[c * per + h, HEAD_DIM:2 * HEAD_DIM, :] = q_tail
    for c in range(WIDTH // cw):
        res = chunk(WIDTH + c * cw)
        for h in range(per):
            head = c * per + h
            a = res[:, h * HEAD_DIM:(h + 1) * HEAD_DIM]
            r = lax.rsqrt(jnp.mean(a * a, axis=-1, keepdims=True) + EPS)
            ko_ref[head, :, 0:HEAD_DIM] = (a * r * k_gain).astype(BF16)
            hi, mid, lo = _bf16_terms(jnp.broadcast_to(nc[:, head:head + 1], (tm, LANES)))
            k_tail = jnp.where(lane == 0, hi, jnp.where(lane == 1, mid, jnp.where(lane == 2, lo, 0.0)))
            ko_ref[head, :, HEAD_DIM:2 * HEAD_DIM] = k_tail.astype(BF16)
    for c in range(WIDTH // cw):
        res = chunk(2 * WIDTH + c * cw)
        for h in range(per):
            vt_ref[c * per + h, 0:HEAD_DIM, :] = res[:, h * HEAD_DIM:(h + 1) * HEAD_DIM].T.astype(BF16)
            vt_ref[c * per + h, HEAD_DIM:V_ROWS, :] = jnp.ones((V_ROWS - HEAD_DIM, tm), BF16)
    for c in range(WIDTH // cw):
        z_ref[:, c * cw:(c + 1) * cw] = chunk(3 * WIDTH + c * cw)


def _proj_fox(y, res, wo, nw, w, fb, qw, kw, batch, seq):
    n, d = res.shape
    d_y = y.shape[1]
    ncol = w.shape[1]
    tm = PROJ_ROWS
    nt = seq // tm

    def row_spec(width):
        return pl.BlockSpec((tm, width), lambda b, t: (b * nt + t, 0))

    def const_spec(shape):
        return pl.BlockSpec(shape, lambda b, t: (0, 0), pipeline_mode=pl.Buffered(1))

    return pl.pallas_call(
        _proj_fox_kernel,
        grid=(batch, nt),
        in_specs=[
            row_spec(d_y), row_spec(d),
            const_spec((d_y, d)), const_spec((1, d)), const_spec((d, ncol)),
            const_spec((1, LANES)), const_spec((1, HEAD_DIM)), const_spec((1, HEAD_DIM)),
        ],
        out_specs=[
            row_spec(d),
            pl.BlockSpec((None, HEADS, 2 * HEAD_DIM, tm), lambda b, t: (b, 0, 0, t)),
            pl.BlockSpec((None, HEADS, tm, 2 * HEAD_DIM), lambda b, t: (b, 0, t, 0)),
            pl.BlockSpec((None, HEADS, V_ROWS, tm), lambda b, t: (b, 0, 0, t)),
            row_spec(WIDTH),
        ],
        out_shape=[
            jax.ShapeDtypeStruct((n, d), F32),
            jax.ShapeDtypeStruct((batch, HEADS, 2 * HEAD_DIM, seq), BF16),
            jax.ShapeDtypeStruct((batch, HEADS, seq, 2 * HEAD_DIM), BF16),
            jax.ShapeDtypeStruct((batch, HEADS, V_ROWS, seq), BF16),
            jax.ShapeDtypeStruct((n, WIDTH), F32),
        ],
        scratch_shapes=[pltpu.VMEM((HEADS, 1), F32)],
        compiler_params=pltpu.CompilerParams(
            dimension_semantics=("parallel", "arbitrary"), vmem_limit_bytes=VMEM_LIMIT_BYTES),
        name="proj_fox",
    )(y, res, wo, nw, w, fb, qw, kw)


def _fox_attn_kernel(qt_ref, k_ref, vt_ref, z_ref, o_ref, m_ref, acc_ref, s_ref, *, blk, nsub, gsz):
    qi = pl.program_id(1)
    m_ref[...] = jnp.full_like(m_ref, -jnp.inf)
    acc_ref[...] = jnp.zeros_like(acc_ref)
    pairs = [(d, c) for d in range(gsz) for c in range(nsub)]
    slots = range(len(pairs))
    diag_groups = nsub // gsz
    assert nsub % gsz == 0 and diag_groups % 2 == 0
    n_main = diag_groups * qi

    def score(g, half, w):
        d, c = pairs[w]
        start = pl.multiple_of((gsz * g + d) * blk, blk)
        s_ref[half, w] = jnp.dot(k_ref[pl.ds(start, blk), :], qt_ref[:, c * blk:(c + 1) * blk],
                                 preferred_element_type=F32)

    def update(g, half, w, masked):
        d, c = pairs[w]
        start = pl.multiple_of((gsz * g + d) * blk, blk)
        s = s_ref[half, w]
        if masked:
            key = lax.broadcasted_iota(jnp.int32, (blk, blk), 0)
            qry = lax.broadcasted_iota(jnp.int32, (blk, blk), 1)
            s = jnp.where(qry >= key, s, -jnp.inf)
        m_prev = m_ref[c]
        m_new = jnp.maximum(m_prev, jnp.max(s, axis=0, keepdims=True))
        alpha = jnp.exp2(m_prev - m_new)
        p = jnp.exp2(s - jnp.tile(m_new, (blk // SUBLANES, 1)))
        acc_ref[c] = acc_ref[c] * jnp.tile(alpha, (V_ROWS // SUBLANES, 1)) + jnp.dot(
            vt_ref[:, pl.ds(start, blk)], p.astype(BF16), preferred_element_type=F32)
        m_ref[c] = m_new

    def group(g, half):
        for w in slots:
            score(g + 1, 1 - half, w)
        for w in slots:
            update(g, half, w, False)

    for w in slots:
        score(0, 0, w)

    def body(i, carry):
        group(2 * i, 0)
        group(2 * i + 1, 1)
        return carry

    lax.fori_loop(0, n_main // 2, body, 0)

    for e in range(diag_groups):
        half = e % 2
        if e + 1 < diag_groups:
            for w, (d, c) in enumerate(pairs):
                if (e + 1) * gsz + d <= c:
                    score(n_main + e + 1, 1 - half, w)
        for w, (d, c) in enumerate(pairs):
            kb = e * gsz + d
            if kb <= c:
                update(n_main + e, half, w, kb == c)

    for c in range(nsub):
        row_sum = acc_ref[c, HEAD_DIM:HEAD_DIM + SUBLANES, :]
        o = (acc_ref[c, 0:HEAD_DIM, :] / jnp.tile(row_sum, (HEAD_DIM // SUBLANES, 1))).T
        z = z_ref[c * blk:(c + 1) * blk, :]
        o_ref[c * blk:(c + 1) * blk, :] = (o * _silu(z)).astype(BF16)


def _fox_attn(q, k, vt, z, batch, seq):
    blk = ATTN_BLOCK
    nsub = ATTN_SUBBLOCKS
    tq = blk * nsub
    nq = seq // tq
    bh = batch * HEADS
    out_spec = pl.BlockSpec((tq, HEAD_DIM), lambda g, i: ((g // HEADS) * nq + i, g % HEADS))
    return pl.pallas_call(
        functools.partial(_fox_attn_kernel, blk=blk, nsub=nsub, gsz=ATTN_GROUP),
        grid=(bh, nq),
        in_specs=[
            pl.BlockSpec((None, 2 * HEAD_DIM, tq), lambda g, i: (g, 0, i)),
            pl.BlockSpec((None, seq, 2 * HEAD_DIM), lambda g, i: (g, 0, 0)),
            pl.BlockSpec((None, V_ROWS, seq), lambda g, i: (g, 0, 0)),
            out_spec,
        ],
        out_specs=out_spec,
        out_shape=jax.ShapeDtypeStruct((batch * seq, WIDTH), BF16),
        scratch_shapes=[
            pltpu.VMEM((nsub, SUBLANES, blk), F32),
            pltpu.VMEM((nsub, V_ROWS, blk), F32),
            pltpu.VMEM((2, ATTN_GROUP * nsub, blk, blk), F32),
        ],
        compiler_params=pltpu.CompilerParams(
            dimension_semantics=("parallel", "arbitrary"), vmem_limit_bytes=VMEM_LIMIT_BYTES),
        name="fox_attn",
    )(q, k, vt, z)


def _gdn_gate_params(a_log, dt_bias):
    prm = jnp.zeros((SUBLANES, LANES), F32)
    prm = prm.at[0, HEADS:2 * HEADS].set(a_log.astype(F32))
    prm = prm.at[1, HEADS:2 * HEADS].set(dt_bias.astype(F32))
    return prm


def kernel(x, a_norm_w, a_w_in, a_conv_w, a_A_log, a_dt_bias, a_o_norm_w, a_w_out, b_norm_w, b_w_in, b_f_bias, b_q_norm_w, b_k_norm_w, b_w_out, final_norm_w):
    batch, seq, d_model = x.shape
    depth = a_norm_w.shape[0] + b_norm_w.shape[0]
    h = x.reshape(batch * seq, d_model)
    final_nw = final_norm_w.reshape(1, d_model)
    pending = None
    for i in range(depth):
        j = i // N_MIXERS
        last_nw = final_nw if i == depth - 1 else None
        if i % N_MIXERS == 0:
            taps = jnp.pad(a_conv_w[j].astype(F32), ((0, SUBLANES - CONV_WIDTH), (0, 0)))
            qkv, z, ba = _proj_gdn(h, a_norm_w[j].reshape(1, d_model),
                                   a_w_in[j].astype(BF16), taps, batch, seq)
            y = _gdn(qkv, z, ba, _gdn_gate_params(a_A_log[j], a_dt_bias[j]),
                     a_o_norm_w[j].reshape(1, HEAD_DIM), batch, seq)
            if i + 1 < depth:
                pending = (y, a_w_out[j].astype(BF16))
            else:
                h = _proj_residual(y, h, a_w_out[j].astype(BF16), last_nw)
        else:
            y_prev, wo_prev = pending
            pending = None
            fb = jnp.pad(b_f_bias[j].astype(F32), (0, LANES - HEADS)).reshape(1, LANES)
            h, qt, k, vt, z = _proj_fox(y_prev, h, wo_prev, b_norm_w[j].reshape(1, d_model),
                                        b_w_in[j].astype(BF16), fb,
                                        b_q_norm_w[j].reshape(1, HEAD_DIM),
                                        b_k_norm_w[j].reshape(1, HEAD_DIM), batch, seq)
            bh = batch * HEADS
            y = _fox_attn(qt.reshape(bh, 2 * HEAD_DIM, seq), k.reshape(bh, seq, 2 * HEAD_DIM),
                          vt.reshape(bh, V_ROWS, seq), z, batch, seq)
            h = _proj_residual(y, h, b_w_out[j].astype(BF16), last_nw)
    return h.reshape(batch, seq, d_model)
```

```python
import functools

import jax
import jax.numpy as jnp
from jax import lax
from jax.experimental import pallas as pl
from jax.experimental.pallas import tpu as pltpu

F32 = jnp.float32
BF16 = jnp.bfloat16
EPS = 1e-6

HEADS = 8
HEAD_DIM = 128
WIDTH = HEADS * HEAD_DIM
CONV_WIDTH = 4
N_MIXERS = 2

LANES = 128
SUBLANES = 8
MXU_COLS = 256
VMEM_LIMIT_BYTES = 56 * 1024 * 1024

GDN_CHUNK = 256
GDN_BASE_BLOCK = 8
PROJ_ROWS = 512
GDN_PROJ_ROWS = 512
RESIDUAL_ROWS = 1024
ATTN_BLOCK = 512
ATTN_SUBBLOCKS = 8
ATTN_GROUP = 1
V_ROWS = HEAD_DIM + 16
LOG2E = 1.4426950408889634

NT_DIMS = (((1,), (1,)), ((), ()))


def _sigmoid(x):
    return 0.5 + 0.5 * jnp.tanh(0.5 * x)


def _silu(x):
    h = 0.5 * x
    return h + h * jnp.tanh(h)


def _softplus(x):
    return jnp.maximum(x, 0.0) + jnp.log(1.0 + jnp.exp(-jnp.abs(x)))


def _mm(a, b):
    return jnp.dot(a.astype(BF16), b.astype(BF16), preferred_element_type=F32)


def _mm_nt(a, b):
    return lax.dot_general(a.astype(BF16), b.astype(BF16), NT_DIMS, preferred_element_type=F32)


def _dot_b(a, b):
    return jnp.dot(a, b, preferred_element_type=F32).astype(BF16)


def _normed_rows(x_ref, nw_ref):
    x = x_ref[...]
    ms = jnp.mean(x * x, axis=-1, keepdims=True)
    return (x * lax.rsqrt(ms + EPS) * nw_ref[...]).astype(BF16)


def _tail_columns(hn, w_ref):
    tail = jnp.dot(hn, w_ref[:, 4 * WIDTH:], preferred_element_type=F32)
    return jnp.concatenate(
        [tail, jnp.zeros((tail.shape[0], LANES - tail.shape[1]), F32)], axis=1)


def _lane_cumsum(x):
    n = x.shape[-1]
    lane = lax.broadcasted_iota(jnp.int32, x.shape, x.ndim - 1)
    shift = 1
    while shift < n:
        x = x + jnp.where(lane >= shift, pltpu.roll(x, shift, x.ndim - 1), 0.0)
        shift *= 2
    return x


def _conv_silu_slab(xs_ref, ys_ref, k, taps, tm):
    nph = (tm + SUBLANES) // SUBLANES
    last = CONV_WIDTH - 1
    taps = [jnp.broadcast_to(taps[s:s + 1, :], (SUBLANES, LANES)) for s in range(CONV_WIDTH)]
    ph = [xs_ref[k, pl.ds(r, SUBLANES, stride=nph), :] for r in range(nph)]
    wrapped = {r: pltpu.roll(ph[r], 1, 0) for r in range(nph - last, nph)}
    for r in range(nph):
        acc = ph[r] * taps[last]
        for s in range(1, CONV_WIDTH):
            src = ph[r - s] if r >= s else wrapped[r - s + nph]
            acc = acc + src * taps[last - s]
        ys_ref[k, pl.ds(r, SUBLANES, stride=nph), :] = _silu(acc)


def _proj_gdn_kernel(x_ref, nw_ref, w_ref, cw_ref, qkv_ref, z_ref, ba_ref, tail_ref, xs_ref, ys_ref):
    tm = x_ref.shape[0]
    cw = MXU_COLS
    per = cw // LANES
    assert (tm + SUBLANES) // SUBLANES % 2 == 1

    @pl.when(pl.program_id(1) == 0)
    def _():
        tail_ref[...] = jnp.zeros_like(tail_ref)

    hn = _normed_rows(x_ref, nw_ref)
    n_conv = 3 * WIDTH // cw

    def chunk_dot(c):
        return jnp.dot(hn, w_ref[:, c * cw:(c + 1) * cw], preferred_element_type=F32)

    pre_next = chunk_dot(0)
    for c in range(n_conv):
        pre = pre_next
        if c + 1 < n_conv:
            pre_next = chunk_dot(c + 1)
        for h in range(per):
            slab = c * per + h
            k = (c % 2) * per + h
            lanes = slice(slab * LANES, (slab + 1) * LANES)
            pre_h = pre[:, h * LANES:(h + 1) * LANES]
            xs_ref[k, 0:SUBLANES, :] = tail_ref[slab]
            xs_ref[k, SUBLANES:SUBLANES + tm, :] = pre_h
            tail_ref[slab] = pre_h[tm - SUBLANES:tm, :]
            _conv_silu_slab(xs_ref, ys_ref, k, cw_ref[:, lanes], tm)
            act = ys_ref[k, SUBLANES:SUBLANES + tm, :]
            if slab < 2 * HEADS:
                scale = HEAD_DIM ** -0.5 if slab < HEADS else 1.0
                r = lax.rsqrt(jnp.sum(act * act, axis=-1, keepdims=True) + EPS)
                act = act * (r * scale)
            qkv_ref[:, lanes] = act.astype(BF16)
    for c in range(WIDTH // cw):
        cols = slice(3 * WIDTH + c * cw, 3 * WIDTH + (c + 1) * cw)
        z_ref[:, c * cw:(c + 1) * cw] = jnp.dot(
            hn, w_ref[:, cols], preferred_element_type=F32).astype(BF16)
    ba_ref[...] = _tail_columns(hn, w_ref)


def _proj_gdn(x, nw, w, cw, batch, seq):
    n, d = x.shape
    ncol = w.shape[1]
    tm = GDN_PROJ_ROWS
    nt = seq // tm

    def row_spec(width):
        return pl.BlockSpec((tm, width), lambda b, t: (b * nt + t, 0))

    def const_spec(shape):
        return pl.BlockSpec(shape, lambda b, t: (0, 0), pipeline_mode=pl.Buffered(1))

    return pl.pallas_call(
        _proj_gdn_kernel,
        grid=(batch, nt),
        in_specs=[
            row_spec(d),
            const_spec((1, d)), const_spec((d, ncol)), const_spec((SUBLANES, 3 * WIDTH)),
        ],
        out_specs=[row_spec(3 * WIDTH), row_spec(WIDTH), row_spec(LANES)],
        out_shape=[
            jax.ShapeDtypeStruct((n, 3 * WIDTH), BF16),
            jax.ShapeDtypeStruct((n, WIDTH), BF16),
            jax.ShapeDtypeStruct((n, LANES), F32),
        ],
        scratch_shapes=[
            pltpu.VMEM((3 * WIDTH // LANES, SUBLANES, LANES), F32),
            pltpu.VMEM((2 * MXU_COLS // LANES, tm + SUBLANES, LANES), F32),
            pltpu.VMEM((2 * MXU_COLS // LANES, tm + SUBLANES, LANES), F32),
        ],
        compiler_params=pltpu.CompilerParams(
            dimension_semantics=("parallel", "arbitrary"), vmem_limit_bytes=VMEM_LIMIT_BYTES),
        name="proj_gdn",
    )(x, nw, w, cw)


def _proj_residual_kernel(y_ref, r_ref, w_ref, o_ref):
    o_ref[...] = r_ref[...] + jnp.dot(y_ref[...], w_ref[...], preferred_element_type=F32)


def _proj_residual_norm_kernel(y_ref, r_ref, w_ref, nw_ref, o_ref):
    h = r_ref[...] + jnp.dot(y_ref[...], w_ref[...], preferred_element_type=F32)
    ms = jnp.mean(h * h, axis=-1, keepdims=True)
    o_ref[...] = h * lax.rsqrt(ms + EPS) * nw_ref[...]


def _proj_residual(y, res, w, final_nw=None):
    n, d_in = y.shape
    d_out = w.shape[1]
    tm = RESIDUAL_ROWS
    in_specs = [
        pl.BlockSpec((tm, d_in), lambda i: (i, 0)),
        pl.BlockSpec((tm, d_out), lambda i: (i, 0)),
        pl.BlockSpec((d_in, d_out), lambda i: (0, 0)),
    ]
    args = [y, res, w]
    body = _proj_residual_kernel
    if final_nw is not None:
        in_specs.append(pl.BlockSpec((1, d_out), lambda i: (0, 0)))
        args.append(final_nw)
        body = _proj_residual_norm_kernel
    return pl.pallas_call(
        body,
        grid=(n // tm,),
        in_specs=in_specs,
        out_specs=pl.BlockSpec((tm, d_out), lambda i: (i, 0)),
        out_shape=jax.ShapeDtypeStruct((n, d_out), F32),
        compiler_params=pltpu.CompilerParams(
            dimension_semantics=("parallel",), vmem_limit_bytes=VMEM_LIMIT_BYTES),
        name="proj_residual",
    )(*args)


def _block_masks(c):
    row = lax.broadcasted_iota(jnp.int32, (c, c), 0)
    col = lax.broadcasted_iota(jnp.int32, (c, c), 1)
    xr = row ^ col
    eye_f = jnp.where(row == col, 1.0, 0.0)
    eye = eye_f.astype(BF16)
    b = GDN_BASE_BLOCK
    below = jnp.where(xr < b, 1.0, 0.0)
    masks = [(below - eye_f).astype(BF16)]
    while b < c:
        below_next = jnp.where(xr < 2 * b, 1.0, 0.0)
        masks.append((below_next - below).astype(BF16))
        below = below_next
        b *= 2
    return eye, masks


def _unit_lower_inverses(a_list, eye, masks):
    c = eye.shape[0]
    ps = [a * masks[0] for a in a_list]
    xs = [eye - p for p in ps]
    width = 2
    while width < GDN_BASE_BLOCK:
        ps = [_dot_b(p, p) for p in ps]
        xs = [x + _dot_b(x, p) for x, p in zip(xs, ps)]
        width *= 2
    b = GDN_BASE_BLOCK
    for m in masks[1:]:
        starts = range(0, c, 2 * b)
        lows = [jnp.concatenate([x[s + b:s + 2 * b, :] for s in starts], axis=0) for x in xs]
        ys = [_dot_b(lo, a * m) for lo, a in zip(lows, a_list)]
        lows = [lo - _dot_b(y, x) for lo, y, x in zip(lows, ys, xs)]
        xs = [jnp.concatenate([piece for k, s in enumerate(starts)
                               for piece in (x[s:s + b, :], lo[k * b:(k + 1) * b, :])], axis=0)
              for x, lo in zip(xs, lows)]
        b *= 2
    return xs


def _gdn_kernel(q_ref, k_ref, v_ref, z_ref, ba_ref, prm_ref, onw_ref, y_ref, s_ref):
    c = q_ref.shape[0]

    @pl.when(pl.program_id(1) == 0)
    def _():
        s_ref[...] = jnp.zeros_like(s_ref)

    row = lax.broadcasted_iota(jnp.int32, (c, c), 0)
    col = lax.broadcasted_iota(jnp.int32, (c, c), 1)
    lower_incl = row >= col
    eye, masks = _block_masks(c)

    ba = ba_ref[...]
    neg_a = jnp.exp(prm_ref[0:1, :]) * (-LOG2E)
    beta_all = _sigmoid(ba)
    glog = neg_a * _softplus(ba + prm_ref[1:2, :])
    gcum_t = _lane_cumsum(glog.T[0:2 * HEADS, :])
    gcum = jnp.concatenate([gcum_t, jnp.zeros((LANES - 2 * HEADS, c), F32)], axis=0).T

    heads = range(HEADS)
    sls = [slice(i * HEAD_DIM, (i + 1) * HEAD_DIM) for i in heads]
    gcols = [gcum[:, HEADS + i:HEADS + i + 1] for i in heads]
    glasts = [gcum[c - 1:c, HEADS + i:HEADS + i + 1] for i in heads]
    vb, kbg, qd, kend, a_l, attn = [], [], [], [], [], []
    for i in heads:
        qn_b = q_ref[:, sls[i]]
        kn_b = k_ref[:, sls[i]]
        qn = qn_b.astype(F32)
        kn = kn_b.astype(F32)
        beta = beta_all[:, i:i + 1]
        eg = jnp.exp2(gcols[i])
        kb = kn * beta
        vb.append(v_ref[:, sls[i]].astype(F32) * beta)
        kbg.append(kb * eg)
        qd.append(qn * eg)
        kend.append(kn * jnp.exp2(glasts[i] - gcols[i]))
        grow = gcum_t[HEADS + i:HEADS + i + 1, :]
        dec = jnp.exp2(jnp.where(lower_incl, gcols[i] - grow, -jnp.inf))
        gram = _mm_nt(jnp.concatenate([kb.astype(BF16), qn_b], axis=0), kn_b)
        a_l.append((gram[:c, :] * dec).astype(BF16))
        attn.append((gram[c:, :] * dec).astype(BF16))
    ns = [x - eye for x in _unit_lower_inverses(a_l, eye, masks)]
    uws = [jnp.dot(ns[i], jnp.concatenate([vb[i], kbg[i]], axis=1).astype(BF16),
                   preferred_element_type=F32) for i in heads]
    us = [vb[i] + uws[i][:, :HEAD_DIM] for i in heads]
    wq = [jnp.concatenate([kbg[i] + uws[i][:, HEAD_DIM:], qd[i]], axis=0).astype(BF16)
          for i in heads]
    ss = [s_ref[i] for i in heads]
    s_bs = [s.astype(BF16) for s in ss]
    zero_s = jnp.zeros((HEAD_DIM, HEAD_DIM), BF16)
    wss = []
    for i in range(0, HEADS, 2):
        s_pair = jnp.concatenate([jnp.concatenate([s_bs[i], zero_s], axis=1),
                                  jnp.concatenate([zero_s, s_bs[i + 1]], axis=1)], axis=0)
        both = jnp.dot(jnp.concatenate([wq[i], wq[i + 1]], axis=1), s_pair,
                       preferred_element_type=F32)
        wss += [both[:, :HEAD_DIM], both[:, HEAD_DIM:]]
    v_new_b = [(us[i] - wss[i][:c, :]).astype(BF16) for i in heads]
    os_ = [wss[i][c:, :] + jnp.dot(attn[i], v_new_b[i], preferred_element_type=F32)
           for i in heads]
    for i in heads:
        s_ref[i] = ss[i] * jnp.exp2(glasts[i]) + _mm(kend[i].T, v_new_b[i])
    for i in heads:
        o = os_[i]
        on = o * lax.rsqrt(jnp.mean(o * o, axis=-1, keepdims=True) + EPS) * onw_ref[...]
        y_ref[:, sls[i]] = (on * _silu(z_ref[:, sls[i]].astype(F32))).astype(BF16)


def _gdn(qkv, z, ba, prm, onw, batch, seq):
    n = qkv.shape[0]
    c = GDN_CHUNK
    nt = seq // c

    def row_spec(width, col):
        return pl.BlockSpec((c, width), lambda b, t: (b * nt + t, col))

    return pl.pallas_call(
        _gdn_kernel,
        grid=(batch, nt),
        in_specs=[
            row_spec(WIDTH, 0), row_spec(WIDTH, 1), row_spec(WIDTH, 2),
            row_spec(WIDTH, 0),
            row_spec(LANES, 0),
            pl.BlockSpec((SUBLANES, LANES), lambda b, t: (0, 0)),
            pl.BlockSpec((1, HEAD_DIM), lambda b, t: (0, 0)),
        ],
        out_specs=row_spec(WIDTH, 0),
        out_shape=jax.ShapeDtypeStruct((n, WIDTH), BF16),
        scratch_shapes=[pltpu.VMEM((HEADS, HEAD_DIM, HEAD_DIM), F32)],
        compiler_params=pltpu.CompilerParams(
            dimension_semantics=("parallel", "arbitrary"), vmem_limit_bytes=VMEM_LIMIT_BYTES),
        name="gdn",
    )(qkv, qkv, qkv, z, ba, prm, onw)


def _bf16_terms(x):
    hi = x.astype(BF16).astype(F32)
    rest = x - hi
    mid = rest.astype(BF16).astype(F32)
    return hi, mid, (rest - mid).astype(BF16).astype(F32)


def _proj_fox_kernel(y_ref, r_ref, wo_ref, nw_ref, w_ref, fb_ref, qw_ref, kw_ref,
                     h_ref, qt_ref, ko_ref, vt_ref, z_ref, carry_ref):
    tm = r_ref.shape[0]
    cw = MXU_COLS
    per = cw // HEAD_DIM

    @pl.when(pl.program_id(1) == 0)
    def _():
        carry_ref[...] = jnp.zeros_like(carry_ref)

    x = r_ref[...] + jnp.dot(y_ref[...], wo_ref[...], preferred_element_type=F32)
    h_ref[...] = x
    ms = jnp.mean(x * x, axis=-1, keepdims=True)
    hn = (x * lax.rsqrt(ms + EPS) * nw_ref[...]).astype(BF16)

    def chunk(col0):
        return jnp.dot(hn, w_ref[:, col0:col0 + cw], preferred_element_type=F32)

    f = _tail_columns(hn, w_ref) + fb_ref[...]
    logf = -_softplus(-f)
    cum = _lane_cumsum(logf.T[0:HEADS, :]) + carry_ref[...]
    carry_ref[...] = cum[:, tm - 1:tm]
    nc = jnp.concatenate([cum * (-LOG2E), jnp.zeros((LANES - HEADS, tm), F32)], axis=0).T

    lane = lax.broadcasted_iota(jnp.int32, (tm, LANES), 1)
    sub = lax.broadcasted_iota(jnp.int32, (HEAD_DIM, tm), 0)
    q_tail = jnp.where(sub < 3, 1.0, 0.0).astype(BF16)
    q_gain = qw_ref[...] * (HEAD_DIM ** -0.5 * LOG2E)
    k_gain = kw_ref[...]
    for c in range(WIDTH // cw):
        res = chunk(c * cw)
        for h in range(per):
            a = res[:, h * HEAD_DIM:(h + 1) * HEAD_DIM]
            r = lax.rsqrt(jnp.mean(a * a, axis=-1, keepdims=True) + EPS)
            qt_ref[c * per + h, 0:HEAD_DIM, :] = (a * r * q_gain).T.astype(BF16)
            qt_ref[c * per + h, HEAD_DIM:2 * HEAD_DIM, :] = q_tail
    for c in range(WIDTH // cw):
        res = chunk(WIDTH + c * cw)
        for h in range(per):
            head = c * per + h
            a = res[:, h * HEAD_DIM:(h + 1) * HEAD_DIM]
            r = lax.rsqrt(jnp.mean(a * a, axis=-1, keepdims=True) + EPS)
            ko_ref[head, :, 0:HEAD_DIM] = (a * r * k_gain).astype(BF16)
            hi, mid, lo = _bf16_terms(jnp.broadcast_to(nc[:, head:head + 1], (tm, LANES)))
            k_tail = jnp.where(lane == 0, hi, jnp.where(lane == 1, mid, jnp.where(lane == 2, lo, 0.0)))
            ko_ref[head, :, HEAD_DIM:2 * HEAD_DIM] = k_tail.astype(BF16)
    for c in range(WIDTH // cw):
        res = chunk(2 * WIDTH + c * cw)
        for h in range(per):
            vt_ref[c * per + h, 0:HEAD_DIM, :] = res[:, h * HEAD_DIM:(h + 1) * HEAD_DIM].T.astype(BF16)
            vt_ref[c * per + h, HEAD_DIM:V_ROWS, :] = jnp.ones((V_ROWS - HEAD_DIM, tm), BF16)
    for c in range(WIDTH // cw):
        z_ref[:, c * cw:(c + 1) * cw] = chunk(3 * WIDTH + c * cw).astype(BF16)


def _proj_fox(y, res, wo, nw, w, fb, qw, kw, batch, seq):
    n, d = res.shape
    d_y = y.shape[1]
    ncol = w.shape[1]
    tm = PROJ_ROWS
    nt = seq // tm

    def row_spec(width):
        return pl.BlockSpec((tm, width), lambda b, t: (b * nt + t, 0))

    def const_spec(shape):
        return pl.BlockSpec(shape, lambda b, t: (0, 0), pipeline_mode=pl.Buffered(1))

    return pl.pallas_call(
        _proj_fox_kernel,
        grid=(batch, nt),
        in_specs=[
            row_spec(d_y), row_spec(d),
            const_spec((d_y, d)), const_spec((1, d)), const_spec((d, ncol)),
            const_spec((1, LANES)), const_spec((1, HEAD_DIM)), const_spec((1, HEAD_DIM)),
        ],
        out_specs=[
            row_spec(d),
            pl.BlockSpec((None, HEADS, 2 * HEAD_DIM, tm), lambda b, t: (b, 0, 0, t)),
            pl.BlockSpec((None, HEADS, tm, 2 * HEAD_DIM), lambda b, t: (b, 0, t, 0)),
            pl.BlockSpec((None, HEADS, V_ROWS, tm), lambda b, t: (b, 0, 0, t)),
            row_spec(WIDTH),
        ],
        out_shape=[
            jax.ShapeDtypeStruct((n, d), F32),
            jax.ShapeDtypeStruct((batch, HEADS, 2 * HEAD_DIM, seq), BF16),
            jax.ShapeDtypeStruct((batch, HEADS, seq, 2 * HEAD_DIM), BF16),
            jax.ShapeDtypeStruct((batch, HEADS, V_ROWS, seq), BF16),
            jax.ShapeDtypeStruct((n, WIDTH), BF16),
        ],
        scratch_shapes=[pltpu.VMEM((HEADS, 1), F32)],
        compiler_params=pltpu.CompilerParams(
            dimension_semantics=("parallel", "arbitrary"), vmem_limit_bytes=VMEM_LIMIT_BYTES),
        name="proj_fox",
    )(y, res, wo, nw, w, fb, qw, kw)


def _fox_attn_kernel(qt_ref, k_ref, vt_ref, z_ref, o_ref, m_ref, acc_ref, s_ref, *, blk, nsub, gsz):
    qi = pl.program_id(1)
    m_ref[...] = jnp.full_like(m_ref, -jnp.inf)
    acc_ref[...] = jnp.zeros_like(acc_ref)
    pairs = [(d, c) for d in range(gsz) for c in range(nsub)]
    slots = range(len(pairs))
    diag_groups = nsub // gsz
    assert nsub % gsz == 0 and diag_groups % 2 == 0
    n_main = diag_groups * qi

    def score(g, half, w):
        d, c = pairs[w]
        start = pl.multiple_of((gsz * g + d) * blk, blk)
        s_ref[half, w] = jnp.dot(k_ref[pl.ds(start, blk), :], qt_ref[:, c * blk:(c + 1) * blk],
                                 preferred_element_type=F32)

    def update(g, half, w, masked):
        d, c = pairs[w]
        start = pl.multiple_of((gsz * g + d) * blk, blk)
        s = s_ref[half, w]
        if masked:
            key = lax.broadcasted_iota(jnp.int32, (blk, blk), 0)
            qry = lax.broadcasted_iota(jnp.int32, (blk, blk), 1)
            s = jnp.where(qry >= key, s, -jnp.inf)
        m_prev = m_ref[c]
        m_new = jnp.maximum(m_prev, jnp.max(s, axis=0, keepdims=True))
        alpha = jnp.exp2(m_prev - m_new)
        p = jnp.exp2(s - jnp.tile(m_new, (blk // SUBLANES, 1)))
        acc_ref[c] = acc_ref[c] * jnp.tile(alpha, (V_ROWS // SUBLANES, 1)) + jnp.dot(
            vt_ref[:, pl.ds(start, blk)], p.astype(BF16), preferred_element_type=F32)
        m_ref[c] = m_new

    def group(g, half):
        for w in slots:
            score(g + 1, 1 - half, w)
        for w in slots:
            update(g, half, w, False)

    for w in slots:
        score(0, 0, w)

    def body(i, carry):
        group(2 * i, 0)
        group(2 * i + 1, 1)
        return carry

    lax.fori_loop(0, n_main // 2, body, 0)

    for e in range(diag_groups):
        half = e % 2
        if e + 1 < diag_groups:
            for w, (d, c) in enumerate(pairs):
                if (e + 1) * gsz + d <= c:
                    score(n_main + e + 1, 1 - half, w)
        for w, (d, c) in enumerate(pairs):
            kb = e * gsz + d
            if kb <= c:
                update(n_main + e, half, w, kb == c)

    for c in range(nsub):
        row_sum = acc_ref[c, HEAD_DIM:HEAD_DIM + SUBLANES, :]
        o = (acc_ref[c, 0:HEAD_DIM, :] / jnp.tile(row_sum, (HEAD_DIM // SUBLANES, 1))).T
        z = z_ref[c * blk:(c + 1) * blk, :].astype(F32)
        o_ref[c * blk:(c + 1) * blk, :] = (o * _silu(z)).astype(BF16)


def _fox_attn(q, k, vt, z, batch, seq):
    blk = ATTN_BLOCK
    nsub = ATTN_SUBBLOCKS
    tq = blk * nsub
    nq = seq // tq
    bh = batch * HEADS
    out_spec = pl.BlockSpec((tq, HEAD_DIM), lambda g, i: ((g // HEADS) * nq + i, g % HEADS))
    return pl.pallas_call(
        functools.partial(_fox_attn_kernel, blk=blk, nsub=nsub, gsz=ATTN_GROUP),
        grid=(bh, nq),
        in_specs=[
            pl.BlockSpec((None, 2 * HEAD_DIM, tq), lambda g, i: (g, 0, i)),
            pl.BlockSpec((None, seq, 2 * HEAD_DIM), lambda g, i: (g, 0, 0)),
            pl.BlockSpec((None, V_ROWS, seq), lambda g, i: (g, 0, 0)),
            out_spec,
        ],
        out_specs=out_spec,
        out_shape=jax.ShapeDtypeStruct((batch * seq, WIDTH), BF16),
        scratch_shapes=[
            pltpu.VMEM((nsub, SUBLANES, blk), F32),
            pltpu.VMEM((nsub, V_ROWS, blk), F32),
            pltpu.VMEM((2, ATTN_GROUP * nsub, blk, blk), F32),
        ],
        compiler_params=pltpu.CompilerParams(
            dimension_semantics=("parallel", "arbitrary"), vmem_limit_bytes=VMEM_LIMIT_BYTES),
        name="fox_attn",
    )(q, k, vt, z)


def _gdn_gate_params(a_log, dt_bias):
    prm = jnp.zeros((SUBLANES, LANES), F32)
    prm = prm.at[0, HEADS:2 * HEADS].set(a_log.astype(F32))
    prm = prm.at[1, HEADS:2 * HEADS].set(dt_bias.astype(F32))
    return prm


def kernel(x, a_norm_w, a_w_in, a_conv_w, a_A_log, a_dt_bias, a_o_norm_w, a_w_out, b_norm_w, b_w_in, b_f_bias, b_q_norm_w, b_k_norm_w, b_w_out, final_norm_w):
    batch, seq, d_model = x.shape
    depth = a_norm_w.shape[0] + b_norm_w.shape[0]
    h = x.reshape(batch * seq, d_model)
    final_nw = final_norm_w.reshape(1, d_model)
    pending = None
    for i in range(depth):
        j = i // N_MIXERS
        last_nw = final_nw if i == depth - 1 else None
        if i % N_MIXERS == 0:
            taps = jnp.pad(a_conv_w[j].astype(F32), ((0, SUBLANES - CONV_WIDTH), (0, 0)))
            qkv, z, ba = _proj_gdn(h, a_norm_w[j].reshape(1, d_model),
                                   a_w_in[j].astype(BF16), taps, batch, seq)
            y = _gdn(qkv, z, ba, _gdn_gate_params(a_A_log[j], a_dt_bias[j]),
                     a_o_norm_w[j].reshape(1, HEAD_DIM), batch, seq)
            if i + 1 < depth:
                pending = (y, a_w_out[j].astype(BF16))
            else:
                h = _proj_residual(y, h, a_w_out[j].astype(BF16), last_nw)
        else:
            y_prev, wo_prev = pending
            pending = None
            fb = jnp.pad(b_f_bias[j].astype(F32), (0, LANES - HEADS)).reshape(1, LANES)
            h, qt, k, vt, z = _proj_fox(y_prev, h, wo_prev, b_norm_w[j].reshape(1, d_model),
                                        b_w_in[j].astype(BF16), fb,
                                        b_q_norm_w[j].reshape(1, HEAD_DIM),
                                        b_k_norm_w[j].reshape(1, HEAD_DIM), batch, seq)
            bh = batch * HEADS
            y = _fox_attn(qt.reshape(bh, 2 * HEAD_DIM, seq), k.reshape(bh, seq, 2 * HEAD_DIM),
                          vt.reshape(bh, V_ROWS, seq), z, batch, seq)
            h = _proj_residual(y, h, b_w_out[j].astype(BF16), last_nw)
    return h.reshape(batch, seq, d_model)
```
